```python
import jax, jax.numpy as jnp
from jax import lax
import numpy as np

D_MODEL = 2048
BATCH = 4
SEQ = 4096
DEPTH = 2

HEAD_DIM = 64
MIX_WIDTH = D_MODEL
A_WIDTH = MIX_WIDTH // 2
A_HEADS = A_WIDTH // HEAD_DIM
A_KV_HEADS = A_HEADS // 4
WINDOW = 128
B_WIDTH = MIX_WIDTH - A_WIDTH
POOL_WINDOWS = (2, 4, 8, 16)
POOL_GROUPS = len(POOL_WINDOWS)
POOL_GW = B_WIDTH // POOL_GROUPS
C_WIDTH = MIX_WIDTH // 2
C_HEADS = 4
C_DV = C_WIDTH // C_HEADS
C_DQK = C_DV // 2
C_CONV = 4
C_CHUNK = 64
FORGET_BIAS = 3.0
D_WIDTH = MIX_WIDTH - C_WIDTH
D_HEADS = D_WIDTH // HEAD_DIM
D_KV_HEADS = D_HEADS // 4
MOBA_BLOCK = 256
MOBA_TOPK = 3
MOBA_QCHUNK = 32
D_FF = 5632
FFN_CONV = 3
NORM_EPS = 1e-6
NEG_INF = -1e30

EVEN_IN = A_WIDTH + 2 * A_KV_HEADS * HEAD_DIM + B_WIDTH
ODD_IN = 2 * C_HEADS * C_DQK + 2 * C_WIDTH + 2 * C_HEADS + D_WIDTH + 2 * D_KV_HEADS * HEAD_DIM
N_EVEN = (DEPTH + 1) // 2
N_ODD = DEPTH // 2

kernel_name = 'hybrid_swa_pool_mlstm_moba_block'


def rms_norm(x, g):
    x32 = x.astype(jnp.float32)
    y = x32 * lax.rsqrt(jnp.mean(x32 * x32, axis=-1, keepdims=True) + NORM_EPS)
    return (y * g.astype(jnp.float32)).astype(x.dtype)


def alibi_slopes(n_heads):
    return jnp.asarray([2.0 ** (-8.0 * (h + 1) / n_heads) for h in range(n_heads)], jnp.float32)


def causal_depthwise_conv(x, w, b):
    K, T = w.shape[0], x.shape[1]
    xp = jnp.pad(x, ((0, 0), (K - 1, 0), (0, 0)))
    y = b
    for j in range(K):
        y = y + xp[:, j:j + T] * w[j]
    return y


def sliding_window_attention(q, k, v, sinks, slopes):
    B, T, H, hd = q.shape
    KV = k.shape[2]
    G = H // KV
    W = WINDOW
    nb = T // W
    qb = q.reshape(B, nb, W, KV, G, hd)

    def band(t):
        tp = jnp.pad(t, ((0, 0), (W, 0), (0, 0), (0, 0))).reshape(B, nb + 1, W, KV, hd)
        return jnp.concatenate([tp[:, :-1], tp[:, 1:]], axis=2)

    kb, vb = band(k), band(v)
    s = jnp.einsum('bnqkgd,bnskd->bnkgqs', qb, kb, preferred_element_type=jnp.float32)
    dist = jnp.arange(W)[:, None] + W - jnp.arange(2 * W)[None, :]
    key_abs = jnp.arange(nb)[:, None] * W + jnp.arange(2 * W)[None, :] - W
    mask = ((dist >= 0) & (dist < W))[None] & (key_abs >= 0)[:, None, :]
    bias = -slopes.reshape(KV, G, 1, 1) * dist.astype(jnp.float32)
    logits = jnp.where(mask[None, :, None, None], s + bias[None, None], NEG_INF)
    sink = jnp.broadcast_to(sinks.astype(jnp.float32).reshape(1, 1, KV, G, 1, 1), logits.shape[:-1] + (1,))
    p = jax.nn.softmax(jnp.concatenate([logits, sink], axis=-1), axis=-1)[..., :-1]
    o = jnp.einsum('bnkgqs,bnskd->bnqkgd', p.astype(v.dtype), vb)
    return o.reshape(B, T, H * hd)


def multiscale_pool_mixer(p, w, b, scale):
    B, T, C = p.shape
    pg = p.reshape(B, T, POOL_GROUPS, POOL_GW).astype(jnp.float32)
    cs = jnp.cumsum(pg, axis=1)
    pos = jnp.arange(1, T + 1, dtype=jnp.float32)
    means = []
    for g, win in enumerate(POOL_WINDOWS):
        c_g = cs[:, :, g]
        lag = jnp.pad(c_g, ((0, 0), (win, 0), (0, 0)))[:, :T]
        means.append((c_g - lag) / jnp.minimum(pos, win)[None, :, None])
    d = (jnp.stack(means, axis=2) - pg).astype(p.dtype)
    y = jnp.einsum('btgc,gcd->btgd', d, w) + b
    return y.reshape(B, T, C) * scale


def mlstm(q, k, v, i_pre, f_pre):
    B, T, H, dqk = q.shape
    dv = v.shape[-1]
    L = C_CHUNK
    nc = T // L
    f32 = jnp.float32

    def chunks(t):
        t = t.astype(f32).reshape((B, nc, L, H) + t.shape[3:])
        return jnp.moveaxis(t, 1, 0).swapaxes(2, 3)

    qs, ks, vs = chunks(q), chunks(k) * (dqk ** -0.5), chunks(v)
    ig = chunks(i_pre)
    lf = chunks(jax.nn.log_sigmoid(f_pre.astype(f32)))
    causal = jnp.tril(jnp.ones((L, L), bool))

    def step(carry, xs):
        C_st, n_st, m = carry
        qc, kc, vc, igc, lfc = xs
        b = jnp.cumsum(lfc, axis=-1)
        logd = jnp.where(causal, b[..., :, None] - b[..., None, :] + igc[..., None, :], -jnp.inf)
        m_inter = m[..., None] + b
        m_row = jnp.maximum(m_inter, jnp.max(logd, axis=-1))
        w_inter = jnp.exp(m_inter - m_row)
        s = jnp.einsum('bhld,bhsd->bhls', qc, kc) * jnp.exp(logd - m_row[..., None])
        num = w_inter[..., None] * jnp.einsum('bhld,bhvd->bhlv', qc, C_st) + jnp.einsum('bhls,bhsv->bhlv', s, vc)
        den = w_inter * jnp.einsum('bhld,bhd->bhl', qc, n_st) + jnp.sum(s, axis=-1)
        h = num / jnp.maximum(jnp.abs(den), jnp.exp(-m_row))[..., None]
        b_last = b[..., -1]
        logw = b_last[..., None] - b + igc
        m_new = jnp.maximum(m + b_last, jnp.max(logw, axis=-1))
        w_old = jnp.exp(m + b_last - m_new)
        ws = jnp.exp(logw - m_new[..., None])
        C_new = w_old[..., None, None] * C_st + jnp.einsum('bhs,bhsv,bhsd->bhvd', ws, vc, kc)
        n_new = w_old[..., None] * n_st + jnp.einsum('bhs,bhsd->bhd', ws, kc)
        return (C_new, n_new, m_new), h

    init = (jnp.zeros((B, H, dv, dqk), f32), jnp.zeros((B, H, dqk), f32), jnp.zeros((B, H), f32))
    _, hs = lax.scan(step, init, (qs, ks, vs, ig, lf))
    return hs.transpose(1, 0, 3, 2, 4).reshape(B, T, H, dv)


def moba_attention(q, k, v, slopes):
    B, T, H, hd = q.shape
    KV = k.shape[2]
    G = H // KV
    BS = MOBA_BLOCK
    QC = MOBA_QCHUNK
    f32 = jnp.float32
    nblk = -(-T // BS)
    Tp = nblk * BS
    pad = ((0, 0), (0, Tp - T), (0, 0), (0, 0))
    q, k, v = jnp.pad(q, pad), jnp.pad(k, pad), jnp.pad(v, pad)
    kb = k.reshape(B, nblk, BS, KV, hd).transpose(0, 3, 1, 2, 4)
    vb = v.reshape(B, nblk, BS, KV, hd).transpose(0, 3, 1, 2, 4)
    k_mean = jnp.mean(kb.astype(f32), axis=3)
    qg = q.reshape(B, Tp, KV, G, hd)
    gate = jnp.einsum('btkgd,bknd->bkgtn', qg.astype(f32), k_mean)
    past = jnp.arange(nblk)[None, :] < (jnp.arange(Tp) // BS)[:, None]
    gate = jnp.where(past, gate, -jnp.inf)
    topk = min(MOBA_TOPK, nblk)
    top_val, top_idx = lax.top_k(gate, topk)
    sel_ok = jnp.isfinite(top_val)
    nqc = Tp // QC
    xs = (qg.reshape(B, nqc, QC, KV, G, hd).transpose(1, 0, 2, 3, 4, 5),
          top_idx.reshape(B, KV, G, nqc, QC, topk).transpose(3, 0, 1, 2, 4, 5),
          sel_ok.reshape(B, KV, G, nqc, QC, topk).transpose(3, 0, 1, 2, 4, 5),
          jnp.arange(nqc) * QC)
    b_ix = jnp.arange(B)[:, None, None, None, None]
    kv_ix = jnp.arange(KV)[None, :, None, None, None]
    offs = jnp.arange(BS)
    slope_sel = slopes.reshape(1, KV, G, 1, 1, 1)
    slope_own = slopes.reshape(1, KV, G, 1, 1)

    def step(args):
        qc, idx, ok, start = args
        qpos = start + jnp.arange(QC)
        kg = kb[b_ix, kv_ix, idx]
        vg = vb[b_ix, kv_ix, idx]
        s_sel = jnp.einsum('bqkgd,bkgqnsd->bkgqns', qc, kg, preferred_element_type=f32)
        pos_sel = idx[..., None] * BS + offs
        s_sel = jnp.where(ok[..., None], s_sel - slope_sel * (qpos[:, None, None] - pos_sel), NEG_INF)
        own = start // BS
        k_own = lax.dynamic_index_in_dim(kb, own, axis=2, keepdims=False)
        v_own = lax.dynamic_index_in_dim(vb, own, axis=2, keepdims=False)
        s_own = jnp.einsum('bqkgd,bksd->bkgqs', qc, k_own, preferred_element_type=f32)
        dist_own = qpos[:, None] - (own * BS + offs)[None, :]
        s_own = jnp.where(dist_own >= 0, s_own - slope_own * dist_own, NEG_INF)
        logits = jnp.concatenate([s_sel.reshape(B, KV, G, QC, topk * BS), s_own], axis=-1)
        p = jax.nn.softmax(logits, axis=-1).astype(v.dtype)
        p_sel = p[..., :topk * BS].reshape(B, KV, G, QC, topk, BS)
        p_own = p[..., topk * BS:]
        return (jnp.einsum('bkgqns,bkgqnsd->bqkgd', p_sel, vg)
                + jnp.einsum('bkgqs,bksd->bqkgd', p_own, v_own))

    out = lax.map(step, xs)
    return out.transpose(1, 0, 2, 3, 4, 5).reshape(B, Tp, H * hd)[:, :T]


def even_mixer(h, w_in, w_out, sinks, pool_w, pool_b, pool_scale, slopes):
    B, T, _ = h.shape
    kvw = A_KV_HEADS * HEAD_DIM
    q, k, v, p = jnp.split(h @ w_in, [A_WIDTH, A_WIDTH + kvw, A_WIDTH + 2 * kvw], axis=-1)
    a = sliding_window_attention(q.reshape(B, T, A_HEADS, HEAD_DIM) * (HEAD_DIM ** -0.5),
                                 k.reshape(B, T, A_KV_HEADS, HEAD_DIM),
                                 v.reshape(B, T, A_KV_HEADS, HEAD_DIM), sinks, slopes)
    bo = multiscale_pool_mixer(p, pool_w, pool_b, pool_scale)
    return jnp.concatenate([a, bo], axis=-1) @ w_out


def odd_mixer(h, w_in, w_out, conv_w, conv_b, gate_b, mh_norm, slopes):
    B, T, _ = h.shape
    kvw = D_KV_HEADS * HEAD_DIM
    sizes = [2 * C_HEADS * C_DQK, C_WIDTH, C_WIDTH, 2 * C_HEADS, D_WIDTH, kvw, kvw]
    qk_c, v_c, o_c, g_c, q_d, k_d, v_d = jnp.split(h @ w_in, np.cumsum(sizes)[:-1].tolist(), axis=-1)
    qk_c = jax.nn.silu(causal_depthwise_conv(qk_c, conv_w, conv_b))
    q_c, k_c = jnp.split(qk_c, 2, axis=-1)
    g_c = g_c + gate_b
    hc = mlstm(q_c.reshape(B, T, C_HEADS, C_DQK), k_c.reshape(B, T, C_HEADS, C_DQK),
               v_c.reshape(B, T, C_HEADS, C_DV), g_c[..., :C_HEADS], g_c[..., C_HEADS:])
    hc = rms_norm(hc, mh_norm.reshape(C_HEADS, C_DV)).astype(h.dtype).reshape(B, T, C_WIDTH)
    hc = hc * jax.nn.sigmoid(o_c)
    hd_out = moba_attention(q_d.reshape(B, T, D_HEADS, HEAD_DIM) * (HEAD_DIM ** -0.5),
                            k_d.reshape(B, T, D_KV_HEADS, HEAD_DIM),
                            v_d.reshape(B, T, D_KV_HEADS, HEAD_DIM), slopes)
    return jnp.concatenate([hc, hd_out], axis=-1) @ w_out


def conv_ffn(h, w_up, conv_w, conv_b, w_down):
    u, g = jnp.split(h @ w_up, 2, axis=-1)
    g = causal_depthwise_conv(g, conv_w, conv_b)
    return (jax.nn.gelu(g, approximate=True) * u) @ w_down


def setup_inputs(seed: int = 0) -> dict:
    key = jax.random.key(seed)
    ks = jax.random.split(key, 25)
    D = D_MODEL

    def nrm(k, shape, s):
        return s * jax.random.normal(k, shape, jnp.float32)

    def gain(k, shape):
        return 1.0 + nrm(k, shape, 0.05)

    return {
        'x': nrm(ks[0], (BATCH, SEQ, D), 1.0),
        'c': nrm(ks[1], (BATCH, D), 1.0),
        'ada_w': nrm(ks[2], (DEPTH, D, 6 * D), D ** -0.5),
        'ada_b': nrm(ks[3], (DEPTH, 6 * D), 0.02),
        'norm_mix_pre': gain(ks[4], (DEPTH, D)),
        'norm_mix_post': gain(ks[5], (DEPTH, D)),
        'norm_ffn_pre': gain(ks[6], (DEPTH, D)),
        'norm_ffn_post': gain(ks[7], (DEPTH, D)),
        'ffn_w_up': nrm(ks[8], (DEPTH, D, 2 * D_FF), D ** -0.5),
        'ffn_conv_w': nrm(ks[9], (DEPTH, FFN_CONV, D_FF), FFN_CONV ** -0.5),
        'ffn_conv_b': nrm(ks[10], (DEPTH, D_FF), 0.02),
        'ffn_w_down': nrm(ks[11], (DEPTH, D_FF, D), D_FF ** -0.5),
        'ev_w_in': nrm(ks[12], (N_EVEN, D, EVEN_IN), D ** -0.5),
        'ev_w_out': nrm(ks[13], (N_EVEN, MIX_WIDTH, D), MIX_WIDTH ** -0.5),
        'ev_sinks': nrm(ks[14], (N_EVEN, A_HEADS), 1.0),
        'ev_pool_w': nrm(ks[15], (N_EVEN, POOL_GROUPS, POOL_GW, POOL_GW), POOL_GW ** -0.5),
        'ev_pool_b': nrm(ks[16], (N_EVEN, POOL_GROUPS, POOL_GW), 0.02),
        'ev_pool_scale': gain(ks[17], (N_EVEN, B_WIDTH)),
        'od_w_in': nrm(ks[18], (N_ODD, D, ODD_IN), D ** -0.5),
        'od_w_out': nrm(ks[19], (N_ODD, MIX_WIDTH, D), MIX_WIDTH ** -0.5),
        'od_conv_w': nrm(ks[20], (N_ODD, C_CONV, 2 * C_HEADS * C_DQK), C_CONV ** -0.5),
        'od_conv_b': nrm(ks[21], (N_ODD, 2 * C_HEADS * C_DQK), 0.02),
        'od_gate_b': jnp.concatenate([nrm(ks[22], (N_ODD, C_HEADS), 0.1),
                                      FORGET_BIAS + nrm(ks[23], (N_ODD, C_HEADS), 0.1)], axis=-1),
        'od_mh_norm': gain(ks[24], (N_ODD, C_WIDTH)),
    }


def reference(x, c, ada_w, ada_b, norm_mix_pre, norm_mix_post, norm_ffn_pre, norm_ffn_post,
              ffn_w_up, ffn_conv_w, ffn_conv_b, ffn_w_down,
              ev_w_in, ev_w_out, ev_sinks, ev_pool_w, ev_pool_b, ev_pool_scale,
              od_w_in, od_w_out, od_conv_w, od_conv_b, od_gate_b, od_mh_norm):
    slopes_a = alibi_slopes(A_HEADS)
    slopes_d = alibi_slopes(D_HEADS)
    c_act = jax.nn.silu(c)
    for layer in range(DEPTH):
        mod = c_act @ ada_w[layer] + ada_b[layer]
        sh_m, sc_m, gt_m, sh_f, sc_f, gt_f = [m[:, None, :] for m in jnp.split(mod, 6, axis=-1)]
        h = rms_norm(x, norm_mix_pre[layer]) * (1.0 + sc_m) + sh_m
        j = layer // 2
        if layer % 2 == 0:
            y = even_mixer(h, ev_w_in[j], ev_w_out[j], ev_sinks[j], ev_pool_w[j], ev_pool_b[j],
                           ev_pool_scale[j], slopes_a)
        else:
            y = odd_mixer(h, od_w_in[j], od_w_out[j], od_conv_w[j], od_conv_b[j], od_gate_b[j],
                          od_mh_norm[j], slopes_d)
        x = x + gt_m * rms_norm(y, norm_mix_post[layer])
        h = rms_norm(x, norm_ffn_pre[layer]) * (1.0 + sc_f) + sh_f
        y = conv_ffn(h, ffn_w_up[layer], ffn_conv_w[layer], ffn_conv_b[layer], ffn_w_down[layer])
        x = x + gt_f * rms_norm(y, norm_ffn_post[layer])
    return x
```

```python
import functools
import math

import jax
import jax.numpy as jnp
from jax import lax
from jax.experimental import pallas as pl
from jax.experimental.pallas import tpu as pltpu

F32 = jnp.float32
BF16 = jnp.bfloat16

HEAD_DIM = 64
GQA_GROUP = 4
SWA_WINDOW = 128
POOL_WINDOWS = (2, 4, 8, 16)
POOL_HALO = 16
MLSTM_HEADS = 4
MLSTM_CONV = 4
MLSTM_CHUNK = 256
MOBA_BLOCK = 256
MOBA_TOPK = 3
FFN_CONV = 3
NORM_EPS = 1e-6
NEG_INF = -1e30
LANES = 128
SUBLANES = 8
VMEM_LIMIT = 56 * 1024 * 1024

NT = (((1,), (1,)), ((), ()))


def _alibi_slopes(n_heads):
    return [2.0 ** (-8.0 * (h + 1) / n_heads) for h in range(n_heads)]


def _params(*semantics):
    return pltpu.CompilerParams(dimension_semantics=semantics, vmem_limit_bytes=VMEM_LIMIT)


def _rms(x, gain):
    ms = jnp.mean(x * x, axis=-1, keepdims=True)
    return x * lax.rsqrt(ms + NORM_EPS) * gain


def _sigmoid(x):
    return 1.0 / (1.0 + jnp.exp(-x))


def _log_sigmoid(x):
    return jnp.minimum(x, 0.0) - jnp.log(1.0 + jnp.exp(-jnp.abs(x)))


def _ada_kernel(c_ref, w_ref, b_ref, o_ref):
    c = c_ref[...]
    a = (c * _sigmoid(c)).astype(BF16)
    o_ref[0] = jnp.dot(a, w_ref[0].astype(BF16), preferred_element_type=F32) + b_ref[0]


def _ada_modulation(c, ada_w, ada_b):
    depth, d, n_out = ada_w.shape
    rows = 2 * SUBLANES
    c_pad = jnp.zeros((rows, d), F32).at[:c.shape[0]].set(c)
    tn = 1024
    out = pl.pallas_call(
        _ada_kernel,
        grid=(depth, n_out // tn),
        in_specs=[
            pl.BlockSpec((rows, d), lambda l, j: (0, 0)),
            pl.BlockSpec((1, d, tn), lambda l, j: (l, 0, j)),
            pl.BlockSpec((1, 1, tn), lambda l, j: (l, 0, j)),
        ],
        out_specs=pl.BlockSpec((1, rows, tn), lambda l, j: (l, 0, j)),
        out_shape=jax.ShapeDtypeStruct((depth, rows, n_out), F32),
        compiler_params=_params("arbitrary", "arbitrary"),
        name="ada_modulation",
    )(c_pad, ada_w, ada_b.reshape(depth, 1, n_out))
    return out[:, :c.shape[0]]


def _modulated_norm(x_ref, g_ref, sc_ref, sh_ref):
    return (_rms(x_ref[...], g_ref[...]) * (1.0 + sc_ref[0]) + sh_ref[0]).astype(BF16)


def _inproj_kernel(x_ref, g_ref, sc_ref, sh_ref, w_ref, o_ref, h_scr):
    @pl.when(pl.program_id(1) == 0)
    def _():
        h_scr[...] = _modulated_norm(x_ref, g_ref, sc_ref, sh_ref)

    o_ref[...] = jnp.dot(h_scr[...], w_ref[...], preferred_element_type=F32)


def _inproj_gates_kernel(x_ref, g_ref, sc_ref, sh_ref, w_ref, wg_ref, o_ref, og_ref, h_scr):
    @pl.when(pl.program_id(1) == 0)
    def _():
        h = _modulated_norm(x_ref, g_ref, sc_ref, sh_ref)
        h_scr[...] = h
        og_ref[...] = jnp.dot(h, wg_ref[...], preferred_element_type=F32)

    o_ref[...] = jnp.dot(h_scr[...], w_ref[...], preferred_element_type=F32)


def _input_projection(x2, seq, gain, scale, shift, w, w_gates=None, *, tm, tn):
    n, d = x2.shape
    n_out = w.shape[1]
    per_seq = seq // tm
    row = lambda i, j: (i, 0)
    mod = lambda i, j: (i // per_seq, 0, 0)
    in_specs = [
        pl.BlockSpec((tm, d), row),
        pl.BlockSpec((1, d), lambda i, j: (0, 0)),
        pl.BlockSpec((1, 1, d), mod),
        pl.BlockSpec((1, 1, d), mod),
        pl.BlockSpec((d, tn), lambda i, j: (0, j)),
    ]
    args = [x2, gain.reshape(1, d), scale, shift, w]
    out_specs = pl.BlockSpec((tm, tn), lambda i, j: (i, j))
    out_shape = jax.ShapeDtypeStruct((n, n_out), F32)
    body = _inproj_kernel
    if w_gates is not None:
        in_specs.append(pl.BlockSpec((d, LANES), lambda i, j: (0, 0)))
        args.append(w_gates)
        out_specs = (out_specs, pl.BlockSpec((tm, LANES), row))
        out_shape = (out_shape, jax.ShapeDtypeStruct((n, LANES), F32))
        body = _inproj_gates_kernel
    return pl.pallas_call(
        body,
        grid=(n // tm, n_out // tn),
        in_specs=in_specs,
        out_specs=out_specs,
        out_shape=out_shape,
        scratch_shapes=[pltpu.VMEM((tm, d), BF16)],
        compiler_params=_params("arbitrary", "arbitrary"),
        name="input_projection",
    )(*args)


def _stack_heads(rows, scale):
    parts = [rows[:, g * HEAD_DIM:(g + 1) * HEAD_DIM] for g in range(GQA_GROUP)]
    return jnp.concatenate(parts, axis=0) * scale


def _swa_kernel(sinks_ref, q_ref, kc_ref, vc_ref, kp_ref, vp_ref, o_ref, *, slopes):
    first_tile = pl.program_id(1) == 0
    w = SWA_WINDOW
    g_rows = GQA_GROUP * w
    kv_heads = kc_ref.shape[1] // HEAD_DIM
    row = lax.broadcasted_iota(jnp.int32, (g_rows, 2 * w), 0)
    ki = lax.broadcasted_iota(jnp.int32, (g_rows, 2 * w), 1)
    dist = (row & (w - 1)) + w - ki
    band = (dist >= 0) & (dist < w)
    first_key = jnp.where(first_tile, w, 0)
    distf = dist.astype(F32)
    head_of_row = row // w
    head_col = lax.broadcasted_iota(jnp.int32, (g_rows, 1), 0) // w
    for kv in range(kv_heads):
        slope = jnp.zeros((g_rows, 2 * w), F32)
        sink = jnp.zeros((g_rows, 1), F32)
        for g in range(GQA_GROUP):
            h = kv * GQA_GROUP + g
            slope = jnp.where(head_of_row == g, slopes[h], slope)
            sink = jnp.where(head_col == g, sinks_ref[h], sink)
        bias = -slope * distf
        lanes = slice(kv * HEAD_DIM, (kv + 1) * HEAD_DIM)
        for sb in range(q_ref.shape[0] // w):
            rows = slice(sb * w, (sb + 1) * w)
            q = _stack_heads(q_ref[rows, kv * GQA_GROUP * HEAD_DIM:(kv + 1) * GQA_GROUP * HEAD_DIM],
                             HEAD_DIM ** -0.5).astype(BF16)
            if sb == 0:
                k2 = jnp.concatenate([kp_ref[:, lanes], kc_ref[0:w, lanes]], axis=0)
                v2 = jnp.concatenate([vp_ref[:, lanes], vc_ref[0:w, lanes]], axis=0)
                mask = band & (ki >= first_key)
            else:
                k2 = kc_ref[(sb - 1) * w:(sb + 1) * w, lanes]
                v2 = vc_ref[(sb - 1) * w:(sb + 1) * w, lanes]
                mask = band
            s = lax.dot_general(q, k2.astype(BF16), NT, preferred_element_type=F32)
            logits = jnp.where(mask, s + bias, NEG_INF)
            m = jnp.maximum(jnp.max(logits, axis=-1, keepdims=True), sink)
            e = jnp.exp(logits - m)
            denom = jnp.sum(e, axis=-1, keepdims=True) + jnp.exp(sink - m)
            o = jnp.dot(e.astype(BF16), v2.astype(BF16), preferred_element_type=F32) / denom
            for g in range(GQA_GROUP):
                h = kv * GQA_GROUP + g
                o_ref[rows, h * HEAD_DIM:(h + 1) * HEAD_DIM] = o[g * w:(g + 1) * w]


def _sliding_window_attention(proj, sinks, batch, seq, *, width, k_col, v_col, tq):
    n = proj.shape[0]
    kvw = width // GQA_GROUP
    per_seq = seq // tq
    sub = tq // SWA_WINDOW
    cur = lambda col: (lambda b, i: (b * per_seq + i, col))
    prev = lambda col: (lambda b, i: (jnp.maximum((b * per_seq + i) * sub - 1, 0), col))
    return pl.pallas_call(
        functools.partial(_swa_kernel, slopes=_alibi_slopes(width // HEAD_DIM)),
        grid=(batch, per_seq),
        in_specs=[
            pl.BlockSpec(memory_space=pltpu.SMEM),
            pl.BlockSpec((tq, width), cur(0)),
            pl.BlockSpec((tq, kvw), cur(k_col)),
            pl.BlockSpec((tq, kvw), cur(v_col)),
            pl.BlockSpec((SWA_WINDOW, kvw), prev(k_col)),
            pl.BlockSpec((SWA_WINDOW, kvw), prev(v_col)),
        ],
        out_specs=pl.BlockSpec((tq, width), cur(0)),
        out_shape=jax.ShapeDtypeStruct((n, width), F32),
        compiler_params=_params("arbitrary", "arbitrary"),
        name="sliding_window_attention",
    )(sinks, proj, proj, proj, proj, proj)


def _pool_kernel(p_ref, halo_ref, w_ref, b_ref, sc_ref, o_ref):
    tt = p_ref.shape[0]
    gw = w_ref.shape[1]
    i = pl.program_id(1)
    pos1 = (i * tt + 1 + lax.broadcasted_iota(jnp.int32, (tt, 1), 0)).astype(F32)
    for g, win in enumerate(POOL_WINDOWS):
        lanes = slice(g * gw, (g + 1) * gw)
        cur = p_ref[:, lanes]
        halo = jnp.where(i > 0, halo_ref[:, lanes], 0.0)
        a = jnp.concatenate([halo, cur], axis=0)
        step = 1
        while step < win:
            a = a[step:] + a[:-step]
            step *= 2
        off = POOL_HALO - (win - 1)
        mean = a[off:off + tt] / jnp.minimum(pos1, float(win))
        d = (mean - cur).astype(BF16)
        y = jnp.dot(d, w_ref[g], preferred_element_type=F32) + b_ref[g]
        o_ref[:, lanes] = y * sc_ref[:, lanes]


def _pool_mixer(proj, pool_w, pool_b, pool_scale, batch, seq, *, col, tt):
    n = proj.shape[0]
    groups, gw, _ = pool_w.shape
    width = groups * gw
    per_seq = seq // tt
    halo_blocks = tt // POOL_HALO
    return pl.pallas_call(
        _pool_kernel,
        grid=(batch, per_seq),
        in_specs=[
            pl.BlockSpec((tt, width), lambda b, i: (b * per_seq + i, col)),
            pl.BlockSpec((POOL_HALO, width),
                         lambda b, i: (jnp.maximum((b * per_seq + i) * halo_blocks - 1, 0), col)),
            pl.BlockSpec((groups, gw, gw), lambda b, i: (0, 0, 0)),
            pl.BlockSpec((groups, 1, gw), lambda b, i: (0, 0, 0)),
            pl.BlockSpec((1, width), lambda b, i: (0, 0)),
        ],
        out_specs=pl.BlockSpec((tt, width), lambda b, i: (b * per_seq + i, 0)),
        out_shape=jax.ShapeDtypeStruct((n, width), F32),
        compiler_params=_params("arbitrary", "arbitrary"),
        name="pool_mixer",
    )(proj, proj, pool_w, pool_b.reshape(groups, 1, gw), pool_scale.reshape(1, width))


def _outproj_kernel(a_ref, b_ref, w_ref, x_ref, gate_ref, g_ref, o_ref):
    half = a_ref.shape[1]
    y = jnp.dot(a_ref[...].astype(BF16), w_ref[0:half, :], preferred_element_type=F32)
    y = y + jnp.dot(b_ref[...].astype(BF16), w_ref[half:, :], preferred_element_type=F32)
    o_ref[...] = x_ref[...] + gate_ref[0] * _rms(y, g_ref[...])


def _output_projection(a, b, w, x2, seq, gate, gain, *, tm):
    n, d = x2.shape
    half = a.shape[1]
    per_seq = seq // tm
    row = lambda i: (i, 0)
    return pl.pallas_call(
        _outproj_kernel,
        grid=(n // tm,),
        in_specs=[
            pl.BlockSpec((tm, half), row),
            pl.BlockSpec((tm, half), row),
            pl.BlockSpec((2 * half, d), lambda i: (0, 0)),
            pl.BlockSpec((tm, d), row),
            pl.BlockSpec((1, 1, d), lambda i: (i // per_seq, 0, 0)),
            pl.BlockSpec((1, d), lambda i: (0, 0)),
        ],
        out_specs=pl.BlockSpec((tm, d), row),
        out_shape=jax.ShapeDtypeStruct((n, d), F32),
        compiler_params=_params("arbitrary"),
        name="output_projection",
    )(a, b, w, x2, gate, gain.reshape(1, d))


def _ffn_kernel(x_ref, g_ref, sc_ref, sh_ref, wu_ref, wg_ref, cw_ref, cb_ref, wd_ref,
                gate_ref, gpost_ref, o_ref, h_scr, acc_scr, carry_scr, *, per_seq):
    i = pl.program_id(0)
    j = pl.program_id(1)
    tm = x_ref.shape[0]

    @pl.when(j == 0)
    def _():
        h_scr[...] = _modulated_norm(x_ref, g_ref, sc_ref, sh_ref)
        acc_scr[...] = jnp.zeros_like(acc_scr)

    h = h_scr[...]
    u = jnp.dot(h, wu_ref[...], preferred_element_type=F32)
    g = jnp.dot(h, wg_ref[...], preferred_element_type=F32)

    @pl.when(i % per_seq == 0)
    def _():
        carry_scr[j] = jnp.zeros(carry_scr.shape[1:], F32)

    tail = carry_scr[j]
    carry_scr[j] = g[tm - SUBLANES:, :]
    row = lax.broadcasted_iota(jnp.int32, g.shape, 0)
    g1 = jnp.where(row == 0, tail[SUBLANES - 1:, :], pltpu.roll(g, 1, 0))
    g2 = jnp.where(row == 0, tail[SUBLANES - 2:SUBLANES - 1, :],
                   jnp.where(row == 1, tail[SUBLANES - 1:, :], pltpu.roll(g, 2, 0)))
    cw = cw_ref[...]
    gc = cb_ref[...] + cw[0:1] * g2 + cw[1:2] * g1 + cw[2:3] * g
    act = 0.5 * gc * (1.0 + jnp.tanh(math.sqrt(2.0 / math.pi) * (gc + 0.044715 * (gc * gc * gc))))
    acc_scr[...] += jnp.dot((act * u).astype(BF16), wd_ref[...], preferred_element_type=F32)

    @pl.when(j == pl.num_programs(1) - 1)
    def _():
        o_ref[...] = x_ref[...] + gate_ref[0] * _rms(acc_scr[...], gpost_ref[...])


def _conv_ffn(x2, seq, gain, scale, shift, w_up, conv_w, conv_b, w_down, gate, gain_post, *, tm, tf):
    n, d = x2.shape
    d_ff = w_down.shape[0]
    nf = d_ff // tf
    per_seq = seq // tm
    row = lambda i, j: (i, 0)
    mod = lambda i, j: (i // per_seq, 0, 0)
    const = lambda i, j: (0, 0)
    return pl.pallas_call(
        functools.partial(_ffn_kernel, per_seq=per_seq),
        grid=(n // tm, nf),
        in_specs=[
            pl.BlockSpec((tm, d), row),
            pl.BlockSpec((1, d), const),
            pl.BlockSpec((1, 1, d), mod),
            pl.BlockSpec((1, 1, d), mod),
            pl.BlockSpec((d, tf), lambda i, j: (0, j)),
            pl.BlockSpec((d, tf), lambda i, j: (0, nf + j)),
            pl.BlockSpec((FFN_CONV, tf), lambda i, j: (0, j)),
            pl.BlockSpec((1, tf), lambda i, j: (0, j)),
            pl.BlockSpec((tf, d), lambda i, j: (j, 0)),
            pl.BlockSpec((1, 1, d), mod),
            pl.BlockSpec((1, d), const),
        ],
        out_specs=pl.BlockSpec((tm, d), row),
        out_shape=jax.ShapeDtypeStruct((n, d), F32),
        scratch_shapes=[
            pltpu.VMEM((tm, d), BF16),
            pltpu.VMEM((tm, d), F32),
            pltpu.VMEM((nf, SUBLANES, tf), F32),
        ],
        compiler_params=_params("arbitrary", "arbitrary"),
        name="conv_ffn",
    )(x2, gain.reshape(1, d), scale, shift, w_up, w_up, conv_w, conv_b.reshape(1, d_ff), w_down,
      gate, gain_post.reshape(1, d))


def _mlstm_kernel(qk_ref, v_ref, o_ref, gt_ref, cw_ref, cb_ref, gb_ref, mh_ref, out_ref,
                  tail_scr, c_scr, n_scr, m_scr):
    L = qk_ref.shape[0]
    heads = MLSTM_HEADS
    dqk = qk_ref.shape[1] // (2 * heads)
    dv = v_ref.shape[1] // heads

    @pl.when(pl.program_id(1) == 0)
    def _():
        tail_scr[...] = jnp.zeros_like(tail_scr)
        c_scr[...] = jnp.zeros_like(c_scr)
        n_scr[...] = jnp.zeros_like(n_scr)
        m_scr[...] = jnp.zeros_like(m_scr)

    raw = qk_ref[...]
    tail = tail_scr[...]
    tail_scr[...] = raw[L - SUBLANES:, :]
    row = lax.broadcasted_iota(jnp.int32, raw.shape, 0)
    cw = cw_ref[...]
    y = cb_ref[...] + cw[MLSTM_CONV - 1:MLSTM_CONV] * raw
    for lag in range(1, MLSTM_CONV):
        shifted = pltpu.roll(raw, lag, 0)
        for r in range(lag):
            shifted = jnp.where(row == r, tail[SUBLANES - lag + r:SUBLANES - lag + r + 1, :], shifted)
        y = y + cw[MLSTM_CONV - 1 - lag:MLSTM_CONV - lag] * shifted
    qk = y * _sigmoid(y)
    q_all = qk[:, :heads * dqk]
    k_all = qk[:, heads * dqk:] * (dqk ** -0.5)

    gates = gt_ref[...] + gb_ref[...]
    lf = _log_sigmoid(gates)
    r_i = lax.broadcasted_iota(jnp.int32, (L, L), 0)
    c_i = lax.broadcasted_iota(jnp.int32, (L, L), 1)
    causal = r_i >= c_i
    tri = causal.astype(F32)
    b_cols = jnp.dot(tri, lf, preferred_element_type=F32, precision=lax.Precision.HIGHEST)
    gates_t = gates.T
    lf_t = _log_sigmoid(gates_t)
    b_rows = lax.dot_general(lf_t, tri, NT, preferred_element_type=F32, precision=lax.Precision.HIGHEST)

    for h in range(heads):
        i_col = gates[:, h:h + 1]
        i_row = gates_t[h:h + 1, :]
        b_col = b_cols[:, heads + h:heads + h + 1]
        b_row = b_rows[heads + h:heads + h + 1, :]
        m_prev = m_scr[h][:, 0:1]
        n_prev = n_scr[h]
        c_prev = c_scr[h]
        qh = q_all[:, h * dqk:(h + 1) * dqk]
        kh = k_all[:, h * dqk:(h + 1) * dqk]
        vh = v_ref[:, h * dv:(h + 1) * dv].astype(BF16)
        qb = qh.astype(BF16)

        logd = jnp.where(causal, b_col - b_row + i_row, -jnp.inf)
        m_inter = m_prev + b_col
        m_row = jnp.maximum(m_inter, jnp.max(logd, axis=-1, keepdims=True))
        w_inter = jnp.exp(m_inter - m_row)
        s = lax.dot_general(qb, kh.astype(BF16), NT, preferred_element_type=F32) * jnp.exp(logd - m_row)
        num = (w_inter * jnp.dot(qb, c_prev.astype(BF16), preferred_element_type=F32)
               + jnp.dot(s.astype(BF16), vh, preferred_element_type=F32))
        den = w_inter * jnp.sum(qh * n_prev, axis=-1, keepdims=True) + jnp.sum(s, axis=-1, keepdims=True)
        hh = num / jnp.maximum(jnp.abs(den), jnp.exp(-m_row))

        b_last = b_col[L - 1:, :]
        logw = b_last - b_col + i_col
        m_new = jnp.maximum(m_prev + b_last, jnp.max(logw, axis=0, keepdims=True))
        w_old = jnp.exp(m_prev + b_last - m_new)
        wk = jnp.exp(logw - m_new) * kh
        c_scr[h] = w_old * c_prev + jnp.dot(wk.T.astype(BF16), vh, preferred_element_type=F32)
        n_scr[h] = w_old * n_prev + jnp.sum(wk, axis=0, keepdims=True)
        m_scr[h] = jnp.broadcast_to(m_new, (1, LANES))

        cols = slice(h * dv, (h + 1) * dv)
        out_ref[:, cols] = _rms(hh, mh_ref[:, cols]) * _sigmoid(o_ref[:, cols])


def _mlstm(proj, gates, conv_w, conv_b, gate_b, mh_norm, batch, seq):
    n = proj.shape[0]
    width = mh_norm.shape[0]
    heads = MLSTM_HEADS
    dqk = width // heads // 2
    L = MLSTM_CHUNK
    per_seq = seq // L
    blk = lambda col: (lambda b, c: (b * per_seq + c, col))
    const = lambda b, c: (0, 0)
    gate_b_pad = jnp.zeros((1, LANES), F32).at[0, :2 * heads].set(gate_b)
    return pl.pallas_call(
        _mlstm_kernel,
        grid=(batch, per_seq),
        in_specs=[
            pl.BlockSpec((L, width), blk(0)),
            pl.BlockSpec((L, width), blk(1)),
            pl.BlockSpec((L, width), blk(2)),
            pl.BlockSpec((L, LANES), blk(0)),
            pl.BlockSpec((MLSTM_CONV, width), const),
            pl.BlockSpec((1, width), const),
            pl.BlockSpec((1, LANES), const),
            pl.BlockSpec((1, width), const),
        ],
        out_specs=pl.BlockSpec((L, width), blk(0)),
        out_shape=jax.ShapeDtypeStruct((n, width), F32),
        scratch_shapes=[
            pltpu.VMEM((SUBLANES, width), F32),
            pltpu.VMEM((heads, dqk, width // heads), F32),
            pltpu.VMEM((heads, 1, dqk), F32),
            pltpu.VMEM((heads, 1, LANES), F32),
        ],
        compiler_params=_params("arbitrary", "arbitrary"),
        name="mlstm",
    )(proj, proj, proj, gates, conv_w, conv_b.reshape(1, width), gate_b_pad, mh_norm.reshape(1, width))


def _block_mean_kernel(k_ref, o_ref):
    nblk = o_ref.shape[1]
    k = k_ref[...]
    o_ref[0] = jnp.mean(k.reshape(nblk, MOBA_BLOCK, k.shape[1]), axis=1)


def _block_means(proj, batch, seq, *, k_col, kvw):
    nblk = seq // MOBA_BLOCK
    return pl.pallas_call(
        _block_mean_kernel,
        grid=(batch,),
        in_specs=[pl.BlockSpec((seq, kvw), lambda b: (b, k_col))],
        out_specs=pl.BlockSpec((1, nblk, kvw), lambda b: (b, 0, 0)),
        out_shape=jax.ShapeDtypeStruct((batch, nblk, kvw), F32),
        compiler_params=_params("arbitrary"),
        name="moba_block_means",
    )(proj)


def _moba_kernel(q_ref, k_ref, v_ref, km_ref, o_ref, qa_scr, m_scr, l_scr, acc_scr, *, slopes):
    n = pl.program_id(1)
    bs = MOBA_BLOCK
    nblk = km_ref.shape[1]
    g_rows = GQA_GROUP * bs
    kv_heads = k_ref.shape[1] // HEAD_DIM

    row = lax.broadcasted_iota(jnp.int32, (g_rows, bs), 0)
    ki = lax.broadcasted_iota(jnp.int32, (g_rows, bs), 1)
    rel = ((row & (bs - 1)) - ki)
    causal_own = rel >= 0
    relf = rel.astype(F32)
    head_of_row = row // bs
    head_col = lax.broadcasted_iota(jnp.int32, (g_rows, 1), 0) // bs
    blk_t = lax.broadcasted_iota(jnp.int32, (nblk, g_rows), 0)
    onehot_col = lax.broadcasted_iota(jnp.int32, (bs, HEAD_DIM), 1)

    for kv in range(kv_heads):
        slope = jnp.zeros((g_rows, bs), F32)
        slope_col = jnp.zeros((g_rows, 1), F32)
        for g in range(GQA_GROUP):
            slope = jnp.where(head_of_row == g, slopes[kv * GQA_GROUP + g], slope)
            slope_col = jnp.where(head_col == g, slopes[kv * GQA_GROUP + g], slope_col)
        rel_bias = slope * relf
        lanes = slice(kv * HEAD_DIM, (kv + 1) * HEAD_DIM)

        qf = _stack_heads(q_ref[:, kv * GQA_GROUP * HEAD_DIM:(kv + 1) * GQA_GROUP * HEAD_DIM],
                          HEAD_DIM ** -0.5)

        gate_t = lax.dot_general(km_ref[0][:, lanes], qf, NT, preferred_element_type=F32,
                                 precision=lax.Precision.HIGHEST)
        rank = jnp.zeros((nblk, g_rows), jnp.int32)
        for c in range(nblk):
            gc = gate_t[c:c + 1, :]
            beats = (gc > gate_t) | ((gc == gate_t) & (c < blk_t))
            rank = rank + jnp.where(beats, jnp.where(c < n, 1, 0), 0)
        chosen = ((blk_t < n) & (rank < MOBA_TOPK)) | (blk_t == n)
        sel_t = jnp.concatenate([jnp.where(chosen, 0.0, NEG_INF),
                                 jnp.full((LANES - nblk, g_rows), NEG_INF, F32)], axis=0)
        sel = sel_t.T[:, 0:HEAD_DIM]
        qa_scr[...] = jnp.concatenate([qf, sel], axis=1).astype(BF16)

        m_scr[...] = jnp.full_like(m_scr, NEG_INF)
        l_scr[...] = jnp.zeros_like(l_scr)
        acc_scr[...] = jnp.zeros_like(acc_scr)

        def attend(j, mask):
            start = pl.multiple_of(j * bs, bs)
            kj = k_ref[pl.ds(start, bs), lanes].astype(BF16)
            vj = v_ref[pl.ds(start, bs), lanes].astype(BF16)
            onehot = jnp.where(onehot_col == j, 1.0, 0.0).astype(BF16)
            k_aug = jnp.concatenate([kj, onehot], axis=1)
            s = lax.dot_general(qa_scr[...], k_aug, NT, preferred_element_type=F32)
            logits = s - rel_bias - slope_col * ((n - j) * bs).astype(F32)
            if mask is not None:
                logits = jnp.where(mask, logits, NEG_INF)
            m_old = m_scr[...]
            m_new = jnp.maximum(m_old, jnp.max(logits, axis=-1, keepdims=True))
            alpha = jnp.exp(m_old - m_new)
            p = jnp.exp(logits - m_new)
            l_scr[...] = alpha * l_scr[...] + jnp.sum(p, axis=-1, keepdims=True)
            acc_scr[...] = alpha * acc_scr[...] + jnp.dot(p.astype(BF16), vj, preferred_element_type=F32)
            m_scr[...] = m_new

        def body(j, carry):
            attend(j, None)
            return carry

        lax.fori_loop(0, n, body, 0)
        attend(n, causal_own)

        o = acc_scr[...] / l_scr[...]
        for g in range(GQA_GROUP):
            h = kv * GQA_GROUP + g
            o_ref[:, h * HEAD_DIM:(h + 1) * HEAD_DIM] = o[g * bs:(g + 1) * bs]


def _moba_attention(proj, batch, seq, *, width, q_col, k_col, v_col):
    n = proj.shape[0]
    kvw = width // GQA_GROUP
    bs = MOBA_BLOCK
    nblk = seq // bs
    k_means = _block_means(proj, batch, seq, k_col=k_col, kvw=kvw)
    g_rows = GQA_GROUP * bs
    return pl.pallas_call(
        functools.partial(_moba_kernel, slopes=_alibi_slopes(width // HEAD_DIM)),
        grid=(batch, nblk),
        in_specs=[
            pl.BlockSpec((bs, width), lambda b, i: (b * nblk + i, q_col)),
            pl.BlockSpec((seq, kvw), lambda b, i: (b, k_col)),
            pl.BlockSpec((seq, kvw), lambda b, i: (b, v_col)),
            pl.BlockSpec((1, nblk, kvw), lambda b, i: (b, 0, 0)),
        ],
        out_specs=pl.BlockSpec((bs, width), lambda b, i: (b * nblk + i, 0)),
        out_shape=jax.ShapeDtypeStruct((n, width), F32),
        scratch_shapes=[
            pltpu.VMEM((g_rows, 2 * HEAD_DIM), BF16),
            pltpu.VMEM((g_rows, 1), F32),
            pltpu.VMEM((g_rows, 1), F32),
            pltpu.VMEM((g_rows, HEAD_DIM), F32),
        ],
        compiler_params=_params("arbitrary", "arbitrary"),
        name="moba_attention",
    )(proj, proj, proj, k_means)


def kernel(x, c, ada_w, ada_b, norm_mix_pre, norm_mix_post, norm_ffn_pre, norm_ffn_post, ffn_w_up, ffn_conv_w, ffn_conv_b, ffn_w_down, ev_w_in, ev_w_out, ev_sinks, ev_pool_w, ev_pool_b, ev_pool_scale, od_w_in, od_w_out, od_conv_w, od_conv_b, od_gate_b, od_mh_norm):
    batch, seq, d = x.shape
    depth = ada_w.shape[0]
    assert seq % 512 == 0 and MOBA_BLOCK <= seq <= MOBA_BLOCK * HEAD_DIM
    x2 = x.reshape(batch * seq, d)

    mod = _ada_modulation(c, ada_w, ada_b)
    mod = mod.reshape(depth, batch, 6, 1, d)

    for layer in range(depth):
        sh_m, sc_m, gt_m, sh_f, sc_f, gt_f = [mod[layer, :, r] for r in range(6)]
        j = layer // 2
        if layer % 2 == 0:
            width = ev_pool_scale.shape[1]
            kvw = width // GQA_GROUP
            w_in = ev_w_in[j]
            w_in = jnp.concatenate([w_in[:, :width], w_in[:, width + 2 * kvw:],
                                    w_in[:, width:width + 2 * kvw]], axis=1).astype(BF16)
            proj = _input_projection(x2, seq, norm_mix_pre[layer], sc_m, sh_m, w_in, tm=512, tn=512)
            a = _sliding_window_attention(proj, ev_sinks[j], batch, seq, width=width,
                                          k_col=2 * width // kvw, v_col=2 * width // kvw + 1, tq=512)
            b = _pool_mixer(proj, ev_pool_w[j].astype(BF16), ev_pool_b[j], ev_pool_scale[j], batch, seq,
                            col=1, tt=512)
            w_out = ev_w_out[j].astype(BF16)
        else:
            width = od_mh_norm.shape[1]
            kvw = width // GQA_GROUP
            heads = MLSTM_HEADS
            w_in = od_w_in[j]
            g0 = 3 * width
            w_main = jnp.concatenate([w_in[:, :g0], w_in[:, g0 + 2 * heads:]], axis=1).astype(BF16)
            w_gates = jnp.zeros((d, LANES), BF16).at[:, :2 * heads].set(
                w_in[:, g0:g0 + 2 * heads].astype(BF16))
            proj, gates = _input_projection(x2, seq, norm_mix_pre[layer], sc_m, sh_m, w_main, w_gates,
                                            tm=512, tn=512)
            a = _mlstm(proj, gates, od_conv_w[j], od_conv_b[j], od_gate_b[j], od_mh_norm[j], batch, seq)
            b = _moba_attention(proj, batch, seq, width=width, q_col=3,
                                k_col=4 * width // kvw, v_col=4 * width // kvw + 1)
            w_out = od_w_out[j].astype(BF16)
        x2 = _output_projection(a, b, w_out, x2, seq, gt_m, norm_mix_post[layer], tm=512)
        x2 = _conv_ffn(x2, seq, norm_ffn_pre[layer], sc_f, sh_f, ffn_w_up[layer].astype(BF16),
                       ffn_conv_w[layer], ffn_conv_b[layer], ffn_w_down[layer].astype(BF16),
                       gt_f, norm_ffn_post[layer], tm=512, tf=512)
    return x2.reshape(batch, seq, d)
```

```python
import functools
import math

import jax
import jax.numpy as jnp
from jax import lax
from jax.experimental import pallas as pl
from jax.experimental.pallas import tpu as pltpu

F32 = jnp.float32
BF16 = jnp.bfloat16

HEAD_DIM = 64
GQA_GROUP = 4
SWA_WINDOW = 128
POOL_WINDOWS = (2, 4, 8, 16)
POOL_HALO = 16
MLSTM_HEADS = 4
MLSTM_CONV = 4
MLSTM_CHUNK = 256
MOBA_BLOCK = 256
MOBA_TOPK = 3
FFN_CONV = 3
FFN_CHUNKS = 2
NORM_EPS = 1e-6
NEG_INF = -1e30
LANES = 128
SUBLANES = 8
VMEM_LIMIT = 56 * 1024 * 1024

NT = (((1,), (1,)), ((), ()))


def _alibi_slopes(n_heads):
    return [2.0 ** (-8.0 * (h + 1) / n_heads) for h in range(n_heads)]


def _params(*semantics):
    return pltpu.CompilerParams(dimension_semantics=semantics, vmem_limit_bytes=VMEM_LIMIT)


def _rms(x, gain):
    ms = jnp.mean(x * x, axis=-1, keepdims=True)
    return x * lax.rsqrt(ms + NORM_EPS) * gain


def _sigmoid(x):
    return 1.0 / (1.0 + jnp.exp(-x))


def _log_sigmoid(x):
    return jnp.minimum(x, 0.0) - jnp.log(1.0 + jnp.exp(-jnp.abs(x)))


def _ada_kernel(c_ref, w_ref, b_ref, o_ref):
    c = c_ref[...]
    a = (c * _sigmoid(c)).astype(BF16)
    o_ref[0] = jnp.dot(a, w_ref[0].astype(BF16), preferred_element_type=F32) + b_ref[0]


def _ada_modulation(c, ada_w, ada_b):
    depth, d, n_out = ada_w.shape
    rows = 2 * SUBLANES
    c_pad = jnp.zeros((rows, d), F32).at[:c.shape[0]].set(c)
    tn = 1024
    out = pl.pallas_call(
        _ada_kernel,
        grid=(depth, n_out // tn),
        in_specs=[
            pl.BlockSpec((rows, d), lambda l, j: (0, 0)),
            pl.BlockSpec((1, d, tn), lambda l, j: (l, 0, j)),
            pl.BlockSpec((1, 1, tn), lambda l, j: (l, 0, j)),
        ],
        out_specs=pl.BlockSpec((1, rows, tn), lambda l, j: (l, 0, j)),
        out_shape=jax.ShapeDtypeStruct((depth, rows, n_out), F32),
        compiler_params=_params("arbitrary", "arbitrary"),
        name="ada_modulation",
    )(c_pad, ada_w, ada_b.reshape(depth, 1, n_out))
    return out[:, :c.shape[0]]


def _modulated_norm(x_ref, g_ref, sc_ref, sh_ref):
    return (_rms(x_ref[...], g_ref[...]) * (1.0 + sc_ref[0]) + sh_ref[0]).astype(BF16)


def _inproj_kernel(x_ref, g_ref, sc_ref, sh_ref, w_ref, o_ref, h_scr):
    @pl.when(pl.program_id(1) == 0)
    def _():
        h_scr[...] = _modulated_norm(x_ref, g_ref, sc_ref, sh_ref)

    o_ref[...] = jnp.dot(h_scr[...], w_ref[...], preferred_element_type=F32)


def _inproj_gates_kernel(x_ref, g_ref, sc_ref, sh_ref, w_ref, wg_ref, o_ref, og_ref, h_scr):
    @pl.when(pl.program_id(1) == 0)
    def _():
        h = _modulated_norm(x_ref, g_ref, sc_ref, sh_ref)
        h_scr[...] = h
        og_ref[...] = jnp.dot(h, wg_ref[...], preferred_element_type=F32)

    o_ref[...] = jnp.dot(h_scr[...], w_ref[...], preferred_element_type=F32)


def _input_projection(x2, seq, gain, scale, shift, w, w_gates=None, *, tm, tn):
    n, d = x2.shape
    n_out = w.shape[1]
    per_seq = seq // tm
    row = lambda i, j: (i, 0)
    mod = lambda i, j: (i // per_seq, 0, 0)
    in_specs = [
        pl.BlockSpec((tm, d), row),
        pl.BlockSpec((1, d), lambda i, j: (0, 0)),
        pl.BlockSpec((1, 1, d), mod),
        pl.BlockSpec((1, 1, d), mod),
        pl.BlockSpec((d, tn), lambda i, j: (0, j)),
    ]
    args = [x2, gain.reshape(1, d), scale, shift, w]
    out_specs = pl.BlockSpec((tm, tn), lambda i, j: (i, j))
    out_shape = jax.ShapeDtypeStruct((n, n_out), F32)
    body = _inproj_kernel
    if w_gates is not None:
        in_specs.append(pl.BlockSpec((d, LANES), lambda i, j: (0, 0)))
        args.append(w_gates)
        out_specs = (out_specs, pl.BlockSpec((tm, LANES), row))
        out_shape = (out_shape, jax.ShapeDtypeStruct((n, LANES), F32))
        body = _inproj_gates_kernel
    return pl.pallas_call(
        body,
        grid=(n // tm, n_out // tn),
        in_specs=in_specs,
        out_specs=out_specs,
        out_shape=out_shape,
        scratch_shapes=[pltpu.VMEM((tm, d), BF16)],
        compiler_params=_params("arbitrary", "arbitrary"),
        name="input_projection",
    )(*args)


def _stack_heads(rows, scale):
    parts = [rows[:, g * HEAD_DIM:(g + 1) * HEAD_DIM] for g in range(GQA_GROUP)]
    return jnp.concatenate(parts, axis=0) * scale


def _swa_kernel(sinks_ref, q_ref, kc_ref, vc_ref, kp_ref, vp_ref, o_ref, *, slopes):
    first_tile = pl.program_id(1) == 0
    w = SWA_WINDOW
    g_rows = GQA_GROUP * w
    kv_heads = kc_ref.shape[1] // HEAD_DIM
    row = lax.broadcasted_iota(jnp.int32, (g_rows, 2 * w), 0)
    ki = lax.broadcasted_iota(jnp.int32, (g_rows, 2 * w), 1)
    dist = (row & (w - 1)) + w - ki
    band = (dist >= 0) & (dist < w)
    first_key = jnp.where(first_tile, w, 0)
    distf = dist.astype(F32)
    head_of_row = row // w
    head_col = lax.broadcasted_iota(jnp.int32, (g_rows, 1), 0) // w
    for kv in range(kv_heads):
        slope = jnp.zeros((g_rows, 2 * w), F32)
        sink = jnp.zeros((g_rows, 1), F32)
        for g in range(GQA_GROUP):
            h = kv * GQA_GROUP + g
            slope = jnp.where(head_of_row == g, slopes[h], slope)
            sink = jnp.where(head_col == g, sinks_ref[h], sink)
        bias = -slope * distf
        lanes = slice(kv * HEAD_DIM, (kv + 1) * HEAD_DIM)
        for sb in range(q_ref.shape[0] // w):
            rows = slice(sb * w, (sb + 1) * w)
            q = _stack_heads(q_ref[rows, kv * GQA_GROUP * HEAD_DIM:(kv + 1) * GQA_GROUP * HEAD_DIM],
                             HEAD_DIM ** -0.5).astype(BF16)
            if sb == 0:
                k2 = jnp.concatenate([kp_ref[:, lanes], kc_ref[0:w, lanes]], axis=0)
                v2 = jnp.concatenate([vp_ref[:, lanes], vc_ref[0:w, lanes]], axis=0)
                mask = band & (ki >= first_key)
            else:
                k2 = kc_ref[(sb - 1) * w:(sb + 1) * w, lanes]
                v2 = vc_ref[(sb - 1) * w:(sb + 1) * w, lanes]
                mask = band
            s = lax.dot_general(q, k2.astype(BF16), NT, preferred_element_type=F32)
            logits = jnp.where(mask, s + bias, NEG_INF)
            m = jnp.maximum(jnp.max(logits, axis=-1, keepdims=True), sink)
            e = jnp.exp(logits - m)
            denom = jnp.sum(e, axis=-1, keepdims=True) + jnp.exp(sink - m)
            o = jnp.dot(e.astype(BF16), v2.astype(BF16), preferred_element_type=F32) / denom
            for g in range(GQA_GROUP):
                h = kv * GQA_GROUP + g
                o_ref[rows, h * HEAD_DIM:(h + 1) * HEAD_DIM] = o[g * w:(g + 1) * w]


def _sliding_window_attention(proj, sinks, batch, seq, *, width, k_col, v_col, tq):
    n = proj.shape[0]
    kvw = width // GQA_GROUP
    per_seq = seq // tq
    sub = tq // SWA_WINDOW
    cur = lambda col: (lambda b, i: (b * per_seq + i, col))
    prev = lambda col: (lambda b, i: (jnp.maximum((b * per_seq + i) * sub - 1, 0), col))
    return pl.pallas_call(
        functools.partial(_swa_kernel, slopes=_alibi_slopes(width // HEAD_DIM)),
        grid=(batch, per_seq),
        in_specs=[
            pl.BlockSpec(memory_space=pltpu.SMEM),
            pl.BlockSpec((tq, width), cur(0)),
            pl.BlockSpec((tq, kvw), cur(k_col)),
            pl.BlockSpec((tq, kvw), cur(v_col)),
            pl.BlockSpec((SWA_WINDOW, kvw), prev(k_col)),
            pl.BlockSpec((SWA_WINDOW, kvw), prev(v_col)),
        ],
        out_specs=pl.BlockSpec((tq, width), cur(0)),
        out_shape=jax.ShapeDtypeStruct((n, width), F32),
        compiler_params=_params("arbitrary", "arbitrary"),
        name="sliding_window_attention",
    )(sinks, proj, proj, proj, proj, proj)


def _pool_kernel(p_ref, halo_ref, w_ref, b_ref, sc_ref, o_ref):
    tt = p_ref.shape[0]
    gw = w_ref.shape[1]
    i = pl.program_id(1)
    pos1 = (i * tt + 1 + lax.broadcasted_iota(jnp.int32, (tt, 1), 0)).astype(F32)
    for g, win in enumerate(POOL_WINDOWS):
        lanes = slice(g * gw, (g + 1) * gw)
        cur = p_ref[:, lanes]
        halo = jnp.where(i > 0, halo_ref[:, lanes], 0.0)
        a = jnp.concatenate([halo, cur], axis=0)
        step = 1
        while step < win:
            a = a[step:] + a[:-step]
            step *= 2
        off = POOL_HALO - (win - 1)
        mean = a[off:off + tt] / jnp.minimum(pos1, float(win))
        d = (mean - cur).astype(BF16)
        y = jnp.dot(d, w_ref[g], preferred_element_type=F32) + b_ref[g]
        o_ref[:, lanes] = y * sc_ref[:, lanes]


def _pool_mixer(proj, pool_w, pool_b, pool_scale, batch, seq, *, col, tt):
    n = proj.shape[0]
    groups, gw, _ = pool_w.shape
    width = groups * gw
    per_seq = seq // tt
    halo_blocks = tt // POOL_HALO
    return pl.pallas_call(
        _pool_kernel,
        grid=(batch, per_seq),
        in_specs=[
            pl.BlockSpec((tt, width), lambda b, i: (b * per_seq + i, col)),
            pl.BlockSpec((POOL_HALO, width),
                         lambda b, i: (jnp.maximum((b * per_seq + i) * halo_blocks - 1, 0), col)),
            pl.BlockSpec((groups, gw, gw), lambda b, i: (0, 0, 0)),
            pl.BlockSpec((groups, 1, gw), lambda b, i: (0, 0, 0)),
            pl.BlockSpec((1, width), lambda b, i: (0, 0)),
        ],
        out_specs=pl.BlockSpec((tt, width), lambda b, i: (b * per_seq + i, 0)),
        out_shape=jax.ShapeDtypeStruct((n, width), F32),
        compiler_params=_params("arbitrary", "arbitrary"),
        name="pool_mixer",
    )(proj, proj, pool_w, pool_b.reshape(groups, 1, gw), pool_scale.reshape(1, width))


def _outproj_kernel(a_ref, b_ref, w_ref, x_ref, gate_ref, g_ref, o_ref):
    half = a_ref.shape[1]
    y = jnp.dot(a_ref[...].astype(BF16), w_ref[0:half, :], preferred_element_type=F32)
    y = y + jnp.dot(b_ref[...].astype(BF16), w_ref[half:, :], preferred_element_type=F32)
    o_ref[...] = x_ref[...] + gate_ref[0] * _rms(y, g_ref[...])


def _output_projection(a, b, w, x2, seq, gate, gain, *, tm):
    n, d = x2.shape
    half = a.shape[1]
    per_seq = seq // tm
    row = lambda i: (i, 0)
    return pl.pallas_call(
        _outproj_kernel,
        grid=(n // tm,),
        in_specs=[
            pl.BlockSpec((tm, half), row),
            pl.BlockSpec((tm, half), row),
            pl.BlockSpec((2 * half, d), lambda i: (0, 0)),
            pl.BlockSpec((tm, d), row),
            pl.BlockSpec((1, 1, d), lambda i: (i // per_seq, 0, 0)),
            pl.BlockSpec((1, d), lambda i: (0, 0)),
        ],
        out_specs=pl.BlockSpec((tm, d), row),
        out_shape=jax.ShapeDtypeStruct((n, d), F32),
        compiler_params=_params("arbitrary"),
        name="output_projection",
    )(a, b, w, x2, gate, gain.reshape(1, d))


def _ffn_kernel(x_ref, g_ref, sc_ref, sh_ref, wu_ref, wg_ref, cw_ref, cb_ref, wd_ref,
                gate_ref, gpost_ref, o_ref, h_scr, acc_scr, carry_scr, *, per_seq):
    i = pl.program_id(0)
    j = pl.program_id(1)
    tm = x_ref.shape[0]

    @pl.when(j == 0)
    def _():
        h_scr[...] = _modulated_norm(x_ref, g_ref, sc_ref, sh_ref)
        acc_scr[...] = jnp.zeros_like(acc_scr)

    @pl.when((j == 0) & (i % per_seq == 0))
    def _():
        carry_scr[...] = jnp.zeros_like(carry_scr)

    h = h_scr[...]
    tf = wd_ref.shape[0]
    chunk = tf // FFN_CHUNKS
    row = lax.broadcasted_iota(jnp.int32, (tm, chunk), 0)
    down = None
    for c in range(FFN_CHUNKS):
        cols = slice(c * chunk, (c + 1) * chunk)
        u = jnp.dot(h, wu_ref[:, cols], preferred_element_type=F32)
        g = jnp.dot(h, wg_ref[:, cols], preferred_element_type=F32)
        tail = carry_scr[j, :, cols]
        carry_scr[j, :, cols] = g[tm - SUBLANES:, :]
        g1 = jnp.where(row == 0, tail[SUBLANES - 1:, :], pltpu.roll(g, 1, 0))
        g2 = jnp.where(row == 0, tail[SUBLANES - 2:SUBLANES - 1, :],
                       jnp.where(row == 1, tail[SUBLANES - 1:, :], pltpu.roll(g, 2, 0)))
        cw = cw_ref[:, cols]
        gc = cb_ref[:, cols] + cw[0:1] * g2 + cw[1:2] * g1 + cw[2:3] * g
        act = 0.5 * gc * (1.0 + jnp.tanh(math.sqrt(2.0 / math.pi) * (gc + 0.044715 * (gc * gc * gc))))
        part = jnp.dot((act * u).astype(BF16), wd_ref[cols, :], preferred_element_type=F32)
        down = part if down is None else down + part
    acc_scr[...] += down

    @pl.when(j == pl.num_programs(1) - 1)
    def _():
        o_ref[...] = x_ref[...] + gate_ref[0] * _rms(acc_scr[...], gpost_ref[...])


def _conv_ffn(x2, seq, gain, scale, shift, w_up, conv_w, conv_b, w_down, gate, gain_post, *, tm, tf):
    n, d = x2.shape
    d_ff = w_down.shape[0]
    nf = d_ff // tf
    per_seq = seq // tm
    row = lambda i, j: (i, 0)
    mod = lambda i, j: (i // per_seq, 0, 0)
    const = lambda i, j: (0, 0)
    return pl.pallas_call(
        functools.partial(_ffn_kernel, per_seq=per_seq),
        grid=(n // tm, nf),
        in_specs=[
            pl.BlockSpec((tm, d), row),
            pl.BlockSpec((1, d), const),
            pl.BlockSpec((1, 1, d), mod),
            pl.BlockSpec((1, 1, d), mod),
            pl.BlockSpec((d, tf), lambda i, j: (0, j)),
            pl.BlockSpec((d, tf), lambda i, j: (0, nf + j)),
            pl.BlockSpec((FFN_CONV, tf), lambda i, j: (0, j)),
            pl.BlockSpec((1, tf), lambda i, j: (0, j)),
            pl.BlockSpec((tf, d), lambda i, j: (j, 0)),
            pl.BlockSpec((1, 1, d), mod),
            pl.BlockSpec((1, d), const),
        ],
        out_specs=pl.BlockSpec((tm, d), row),
        out_shape=jax.ShapeDtypeStruct((n, d), F32),
        scratch_shapes=[
            pltpu.VMEM((tm, d), BF16),
            pltpu.VMEM((tm, d), F32),
            pltpu.VMEM((nf, SUBLANES, tf), F32),
        ],
        compiler_params=_params("arbitrary", "arbitrary"),
        name="conv_ffn",
    )(x2, gain.reshape(1, d), scale, shift, w_up, w_up, conv_w, conv_b.reshape(1, d_ff), w_down,
      gate, gain_post.reshape(1, d))


def _mlstm_kernel(qk_ref, v_ref, o_ref, gt_ref, cw_ref, cb_ref, gb_ref, mh_ref, out_ref,
                  tail_scr, c_scr, n_scr, m_scr):
    L = qk_ref.shape[0]
    heads = MLSTM_HEADS
    dqk = qk_ref.shape[1] // (2 * heads)
    dv = v_ref.shape[1] // heads

    @pl.when(pl.program_id(1) == 0)
    def _():
        tail_scr[...] = jnp.zeros_like(tail_scr)
        c_scr[...] = jnp.zeros_like(c_scr)
        n_scr[...] = jnp.zeros_like(n_scr)
        m_scr[...] = jnp.zeros_like(m_scr)

    raw = qk_ref[...]
    tail = tail_scr[...]
    tail_scr[...] = raw[L - SUBLANES:, :]
    row = lax.broadcasted_iota(jnp.int32, raw.shape, 0)
    cw = cw_ref[...]
    y = cb_ref[...] + cw[MLSTM_CONV - 1:MLSTM_CONV] * raw
    for lag in range(1, MLSTM_CONV):
        shifted = pltpu.roll(raw, lag, 0)
        for r in range(lag):
            shifted = jnp.where(row == r, tail[SUBLANES - lag + r:SUBLANES - lag + r + 1, :], shifted)
        y = y + cw[MLSTM_CONV - 1 - lag:MLSTM_CONV - lag] * shifted
    qk = y * _sigmoid(y)
    q_all = qk[:, :heads * dqk]
    k_all = qk[:, heads * dqk:] * (dqk ** -0.5)

    gates = gt_ref[...] + gb_ref[...]
    lf = _log_sigmoid(gates)
    r_i = lax.broadcasted_iota(jnp.int32, (L, L), 0)
    c_i = lax.broadcasted_iota(jnp.int32, (L, L), 1)
    causal = r_i >= c_i
    tri = causal.astype(F32)
    b_cols = jnp.dot(tri, lf, preferred_element_type=F32, precision=lax.Precision.HIGHEST)
    gates_t = gates.T
    lf_t = _log_sigmoid(gates_t)
    b_rows = lax.dot_general(lf_t, tri, NT, preferred_element_type=F32, precision=lax.Precision.HIGHEST)

    for h in range(heads):
        i_col = gates[:, h:h + 1]
        i_row = gates_t[h:h + 1, :]
        b_col = b_cols[:, heads + h:heads + h + 1]
        b_row = b_rows[heads + h:heads + h + 1, :]
        m_prev = m_scr[h][:, 0:1]
        n_prev = n_scr[h]
        c_prev = c_scr[h]
        qh = q_all[:, h * dqk:(h + 1) * dqk]
        kh = k_all[:, h * dqk:(h + 1) * dqk]
        vh = v_ref[:, h * dv:(h + 1) * dv].astype(BF16)
        qb = qh.astype(BF16)

        logd = jnp.where(causal, b_col - b_row + i_row, -jnp.inf)
        m_inter = m_prev + b_col
        m_row = jnp.maximum(m_inter, jnp.max(logd, axis=-1, keepdims=True))
        w_inter = jnp.exp(m_inter - m_row)
        s = lax.dot_general(qb, kh.astype(BF16), NT, preferred_element_type=F32) * jnp.exp(logd - m_row)
        num = (w_inter * jnp.dot(qb, c_prev.astype(BF16), preferred_element_type=F32)
               + jnp.dot(s.astype(BF16), vh, preferred_element_type=F32))
        den = w_inter * jnp.sum(qh * n_prev, axis=-1, keepdims=True) + jnp.sum(s, axis=-1, keepdims=True)
        hh = num / jnp.maximum(jnp.abs(den), jnp.exp(-m_row))

        b_last = b_col[L - 1:, :]
        logw = b_last - b_col + i_col
        m_new = jnp.maximum(m_prev + b_last, jnp.max(logw, axis=0, keepdims=True))
        w_old = jnp.exp(m_prev + b_last - m_new)
        wk = jnp.exp(logw - m_new) * kh
        c_scr[h] = w_old * c_prev + jnp.dot(wk.T.astype(BF16), vh, preferred_element_type=F32)
        n_scr[h] = w_old * n_prev + jnp.sum(wk, axis=0, keepdims=True)
        m_scr[h] = jnp.broadcast_to(m_new, (1, LANES))

        cols = slice(h * dv, (h + 1) * dv)
        out_ref[:, cols] = _rms(hh, mh_ref[:, cols]) * _sigmoid(o_ref[:, cols])


def _mlstm(proj, gates, conv_w, conv_b, gate_b, mh_norm, batch, seq):
    n = proj.shape[0]
    width = mh_norm.shape[0]
    heads = MLSTM_HEADS
    dqk = width // heads // 2
    L = MLSTM_CHUNK
    per_seq = seq // L
    blk = lambda col: (lambda b, c: (b * per_seq + c, col))
    const = lambda b, c: (0, 0)
    gate_b_pad = jnp.zeros((1, LANES), F32).at[0, :2 * heads].set(gate_b)
    return pl.pallas_call(
        _mlstm_kernel,
        grid=(batch, per_seq),
        in_specs=[
            pl.BlockSpec((L, width), blk(0)),
            pl.BlockSpec((L, width), blk(1)),
            pl.BlockSpec((L, width), blk(2)),
            pl.BlockSpec((L, LANES), blk(0)),
            pl.BlockSpec((MLSTM_CONV, width), const),
            pl.BlockSpec((1, width), const),
            pl.BlockSpec((1, LANES), const),
            pl.BlockSpec((1, width), const),
        ],
        out_specs=pl.BlockSpec((L, width), blk(0)),
        out_shape=jax.ShapeDtypeStruct((n, width), F32),
        scratch_shapes=[
            pltpu.VMEM((SUBLANES, width), F32),
            pltpu.VMEM((heads, dqk, width // heads), F32),
            pltpu.VMEM((heads, 1, dqk), F32),
            pltpu.VMEM((heads, 1, LANES), F32),
        ],
        compiler_params=_params("arbitrary", "arbitrary"),
        name="mlstm",
    )(proj, proj, proj, gates, conv_w, conv_b.reshape(1, width), gate_b_pad, mh_norm.reshape(1, width))


MOBA_SPLIT = 3
MOBA_EXTRA = HEAD_DIM


def _split_bf16(x):
    terms = []
    for _ in range(MOBA_SPLIT - 1):
        t = x.astype(BF16).astype(F32)
        terms.append(t)
        x = x - t
    terms.append(x)
    return terms


def _moba_prepare_kernel(k_ref, v_ref, km_ref, ka_ref, vt_ref):
    nblk = km_ref.shape[1]
    bs = MOBA_BLOCK
    seq, kvw = k_ref.shape
    k = k_ref[...]
    km_ref[0] = jnp.mean(k.reshape(nblk, bs, kvw), axis=1)
    col = lax.broadcasted_iota(jnp.int32, (seq, MOBA_EXTRA), 1)
    row = lax.broadcasted_iota(jnp.int32, (seq, MOBA_EXTRA), 0)
    in_onehot = col < MOBA_SPLIT * nblk
    in_offset = (col >= MOBA_SPLIT * nblk) & (col < MOBA_SPLIT * (nblk + 1))
    onehot = jnp.where(in_onehot & (col % nblk == row // bs), 1.0, 0.0)
    extra = jnp.where(in_offset, (row % bs).astype(F32), onehot)
    for kv in range(kvw // HEAD_DIM):
        ka_ref[0, kv] = jnp.concatenate([k[:, kv * HEAD_DIM:(kv + 1) * HEAD_DIM], extra], axis=1).astype(BF16)
    for j in range(nblk):
        vt_ref[0, j] = v_ref[j * bs:(j + 1) * bs, :].T.astype(BF16)


def _moba_prepare(proj, batch, seq, *, k_col, v_col, kvw):
    nblk = seq // MOBA_BLOCK
    kv_heads = kvw // HEAD_DIM
    return pl.pallas_call(
        _moba_prepare_kernel,
        grid=(batch,),
        in_specs=[pl.BlockSpec((seq, kvw), lambda b: (b, k_col)),
                  pl.BlockSpec((seq, kvw), lambda b: (b, v_col))],
        out_specs=(pl.BlockSpec((1, nblk, kvw), lambda b: (b, 0, 0)),
                   pl.BlockSpec((1, kv_heads, seq, HEAD_DIM + MOBA_EXTRA), lambda b: (b, 0, 0, 0)),
                   pl.BlockSpec((1, nblk, kvw, MOBA_BLOCK), lambda b: (b, 0, 0, 0))),
        out_shape=(jax.ShapeDtypeStruct((batch, nblk, kvw), F32),
                   jax.ShapeDtypeStruct((batch, kv_heads, seq, HEAD_DIM + MOBA_EXTRA), BF16),
                   jax.ShapeDtypeStruct((batch, nblk, kvw, MOBA_BLOCK), BF16)),
        compiler_params=_params("arbitrary"),
        name="moba_prepare",
    )(proj, proj)


def _moba_kernel(q_ref, ka_ref, vt_ref, km_ref, o_ref, qa_scr, acc_scr, ot_scr, *, slopes):
    n = pl.program_id(1)
    bs = MOBA_BLOCK
    nblk = km_ref.shape[1]
    kv_heads = ka_ref.shape[1]
    cols = GQA_GROUP * bs

    lane = lax.broadcasted_iota(jnp.int32, (1, cols), 1)
    head_of_lane = lane // bs
    blk_t = lax.broadcasted_iota(jnp.int32, (nblk, cols), 0)
    dist0 = ((n - blk_t) * bs + (lane & (bs - 1))).astype(F32)
    key = lax.broadcasted_iota(jnp.int32, (bs, cols), 0)
    causal_own = (lax.broadcasted_iota(jnp.int32, (bs, cols), 1) & (bs - 1)) >= key
    sub = lax.broadcasted_iota(jnp.int32, (SUBLANES, cols), 0)
    pad = jnp.zeros((MOBA_EXTRA - MOBA_SPLIT * nblk - SUBLANES, cols), F32)

    q_t = q_ref[...].T

    for kv in range(kv_heads):
        slope = jnp.zeros((1, cols), F32)
        for g in range(GQA_GROUP):
            slope = jnp.where(head_of_lane == g, slopes[kv * GQA_GROUP + g], slope)
        heads = [q_t[(kv * GQA_GROUP + g) * HEAD_DIM:(kv * GQA_GROUP + g + 1) * HEAD_DIM, :]
                 for g in range(GQA_GROUP)]
        qf_t = jnp.concatenate(heads, axis=1) * HEAD_DIM ** -0.5

        gate_t = jnp.dot(km_ref[0][:, kv * HEAD_DIM:(kv + 1) * HEAD_DIM], qf_t,
                         preferred_element_type=F32, precision=lax.Precision.HIGHEST)
        rank = jnp.zeros((nblk, cols), jnp.int32)
        for c in range(nblk):
            gc = gate_t[c:c + 1, :]
            beats = (gc > gate_t) | ((gc == gate_t) & (c < blk_t))
            rank = rank + jnp.where(beats, jnp.where(c < n, 1, 0), 0)
        chosen = ((blk_t < n) & (rank < MOBA_TOPK)) | (blk_t == n)
        bias_t = jnp.where(chosen, 0.0, NEG_INF) - slope * dist0
        slope_rows = jnp.zeros((SUBLANES, cols), F32)
        for t, term in enumerate(_split_bf16(slope)):
            slope_rows = jnp.where(sub == t, term, slope_rows)
        qa_scr[...] = jnp.concatenate([qf_t] + _split_bf16(bias_t) + [slope_rows, pad], axis=0).astype(BF16)
        acc_scr[...] = jnp.zeros_like(acc_scr)

        def attend(j, m_old, l_old, mask):
            start = pl.multiple_of(j * bs, bs)
            s = jnp.dot(ka_ref[0, kv, pl.ds(start, bs), :], qa_scr[...], preferred_element_type=F32)
            if mask is not None:
                s = jnp.where(mask, s, NEG_INF)
            m_new = jnp.maximum(m_old, jnp.max(s, axis=0, keepdims=True))
            alpha = jnp.exp(m_old - m_new)
            p = jnp.exp(s - m_new)
            v_t = vt_ref[0, j, kv * HEAD_DIM:(kv + 1) * HEAD_DIM, :]
            acc_scr[...] = alpha * acc_scr[...] + jnp.dot(v_t, p.astype(BF16), preferred_element_type=F32)
            return m_new, alpha * l_old + jnp.sum(p, axis=0, keepdims=True)

        stats = (jnp.full((1, cols), NEG_INF, F32), jnp.zeros((1, cols), F32))
        stats = lax.fori_loop(0, n, lambda j, c: attend(j, c[0], c[1], None), stats)
        _, l = attend(n, stats[0], stats[1], causal_own)

        o_t = acc_scr[...] / l
        for g in range(GQA_GROUP):
            h = kv * GQA_GROUP + g
            ot_scr[h * HEAD_DIM:(h + 1) * HEAD_DIM, :] = o_t[:, g * bs:(g + 1) * bs]

    o_ref[...] = ot_scr[...].T


def _moba_attention(proj, batch, seq, *, width, q_col, k_col, v_col):
    n = proj.shape[0]
    kvw = width // GQA_GROUP
    kv_heads = kvw // HEAD_DIM
    bs = MOBA_BLOCK
    nblk = seq // bs
    assert MOBA_SPLIT * nblk + SUBLANES <= MOBA_EXTRA
    k_means, k_aug, v_t = _moba_prepare(proj, batch, seq, k_col=k_col, v_col=v_col, kvw=kvw)
    cols = GQA_GROUP * bs
    return pl.pallas_call(
        functools.partial(_moba_kernel, slopes=_alibi_slopes(width // HEAD_DIM)),
        grid=(batch, nblk),
        in_specs=[
            pl.BlockSpec((bs, width), lambda b, i: (b * nblk + i, q_col)),
            pl.BlockSpec((1, kv_heads, seq, HEAD_DIM + MOBA_EXTRA), lambda b, i: (b, 0, 0, 0)),
            pl.BlockSpec((1, nblk, kvw, bs), lambda b, i: (b, 0, 0, 0)),
            pl.BlockSpec((1, nblk, kvw), lambda b, i: (b, 0, 0)),
        ],
        out_specs=pl.BlockSpec((bs, width), lambda b, i: (b * nblk + i, 0)),
        out_shape=jax.ShapeDtypeStruct((n, width), F32),
        scratch_shapes=[
            pltpu.VMEM((HEAD_DIM + MOBA_EXTRA, cols), BF16),
            pltpu.VMEM((HEAD_DIM, cols), F32),
            pltpu.VMEM((width, bs), F32),
        ],
        compiler_params=_params("arbitrary", "arbitrary"),
        name="moba_attention",
    )(proj, k_aug, v_t, k_means)


def kernel(x, c, ada_w, ada_b, norm_mix_pre, norm_mix_post, norm_ffn_pre, norm_ffn_post, ffn_w_up, ffn_conv_w, ffn_conv_b, ffn_w_down, ev_w_in, ev_w_out, ev_sinks, ev_pool_w, ev_pool_b, ev_pool_scale, od_w_in, od_w_out, od_conv_w, od_conv_b, od_gate_b, od_mh_norm):
    batch, seq, d = x.shape
    depth = ada_w.shape[0]
    assert seq % 512 == 0
    x2 = x.reshape(batch * seq, d)

    mod = _ada_modulation(c, ada_w, ada_b)
    mod = mod.reshape(depth, batch, 6, 1, d)

    for layer in range(depth):
        sh_m, sc_m, gt_m, sh_f, sc_f, gt_f = [mod[layer, :, r] for r in range(6)]
        j = layer // 2
        if layer % 2 == 0:
            width = ev_pool_scale.shape[1]
            kvw = width // GQA_GROUP
            w_in = ev_w_in[j]
            w_in = jnp.concatenate([w_in[:, :width], w_in[:, width + 2 * kvw:],
                                    w_in[:, width:width + 2 * kvw]], axis=1).astype(BF16)
            proj = _input_projection(x2, seq, norm_mix_pre[layer], sc_m, sh_m, w_in, tm=1024, tn=1280)
            a = _sliding_window_attention(proj, ev_sinks[j], batch, seq, width=width,
                                          k_col=2 * width // kvw, v_col=2 * width // kvw + 1, tq=512)
            b = _pool_mixer(proj, ev_pool_w[j].astype(BF16), ev_pool_b[j], ev_pool_scale[j], batch, seq,
                            col=1, tt=512)
            w_out = ev_w_out[j].astype(BF16)
        else:
            width = od_mh_norm.shape[1]
            kvw = width // GQA_GROUP
            heads = MLSTM_HEADS
            w_in = od_w_in[j]
            g0 = 3 * width
            w_main = jnp.concatenate([w_in[:, :g0], w_in[:, g0 + 2 * heads:]], axis=1).astype(BF16)
            w_gates = jnp.zeros((d, LANES), BF16).at[:, :2 * heads].set(
                w_in[:, g0:g0 + 2 * heads].astype(BF16))
            proj, gates = _input_projection(x2, seq, norm_mix_pre[layer], sc_m, sh_m, w_main, w_gates,
                                            tm=1024, tn=1536)
            a = _mlstm(proj, gates, od_conv_w[j], od_conv_b[j], od_gate_b[j], od_mh_norm[j], batch, seq)
            b = _moba_attention(proj, batch, seq, width=width, q_col=3,
                                k_col=4 * width // kvw, v_col=4 * width // kvw + 1)
            w_out = od_w_out[j].astype(BF16)
        x2 = _output_projection(a, b, w_out, x2, seq, gt_m, norm_mix_post[layer], tm=512)
        x2 = _conv_ffn(x2, seq, norm_ffn_pre[layer], sc_f, sh_f, ffn_w_up[layer].astype(BF16),
                       ffn_conv_w[layer], ffn_conv_b[layer], ffn_w_down[layer].astype(BF16),
                       gt_f, norm_ffn_post[layer], tm=512, tf=512)
    return x2.reshape(batch, seq, d)
```

```python
import functools
import math

import jax
import jax.numpy as jnp
from jax import lax
from jax.experimental import pallas as pl
from jax.experimental.pallas import tpu as pltpu

F32 = jnp.float32
BF16 = jnp.bfloat16

HEAD_DIM = 64
GQA_GROUP = 4
SWA_WINDOW = 128
POOL_WINDOWS = (2, 4, 8, 16)
POOL_HALO = 16
MLSTM_HEADS = 4
MLSTM_CONV = 4
MLSTM_CHUNK = 256
MOBA_BLOCK = 256
MOBA_TOPK = 3
FFN_CONV = 3
FFN_CHUNK = 256
NORM_EPS = 1e-6
NEG_INF = -1e30
LANES = 128
SUBLANES = 8
VMEM_LIMIT = 60 * 1024 * 1024

NT = (((1,), (1,)), ((), ()))


def _alibi_slopes(n_heads):
    return [2.0 ** (-8.0 * (h + 1) / n_heads) for h in range(n_heads)]


def _params(*semantics):
    return pltpu.CompilerParams(dimension_semantics=semantics, vmem_limit_bytes=VMEM_LIMIT)


def _rms(x, gain):
    ms = jnp.mean(x * x, axis=-1, keepdims=True)
    return x * lax.rsqrt(ms + NORM_EPS) * gain


def _sigmoid(x):
    return 1.0 / (1.0 + jnp.exp(-x))


def _log_sigmoid(x):
    return jnp.minimum(x, 0.0) - jnp.log(1.0 + jnp.exp(-jnp.abs(x)))


def _ada_kernel(c_ref, w_ref, b_ref, o_ref):
    c = c_ref[...]
    a = (c * _sigmoid(c)).astype(BF16)
    o_ref[0] = jnp.dot(a, w_ref[0].astype(BF16), preferred_element_type=F32) + b_ref[0]


def _ada_modulation(c, ada_w, ada_b):
    depth, d, n_out = ada_w.shape
    rows = 2 * SUBLANES
    c_pad = jnp.zeros((rows, d), F32).at[:c.shape[0]].set(c)
    tn = 1024
    out = pl.pallas_call(
        _ada_kernel,
        grid=(depth, n_out // tn),
        in_specs=[
            pl.BlockSpec((rows, d), lambda l, j: (0, 0)),
            pl.BlockSpec((1, d, tn), lambda l, j: (l, 0, j)),
            pl.BlockSpec((1, 1, tn), lambda l, j: (l, 0, j)),
        ],
        out_specs=pl.BlockSpec((1, rows, tn), lambda l, j: (l, 0, j)),
        out_shape=jax.ShapeDtypeStruct((depth, rows, n_out), F32),
        compiler_params=_params("arbitrary", "arbitrary"),
        name="ada_modulation",
    )(c_pad, ada_w, ada_b.reshape(depth, 1, n_out))
    return out[:, :c.shape[0]]


def _modulated_norm(x_ref, g_ref, sc_ref, sh_ref):
    return (_rms(x_ref[...], g_ref[...]) * (1.0 + sc_ref[0]) + sh_ref[0]).astype(BF16)


def _inproj_kernel(x_ref, g_ref, sc_ref, sh_ref, w_ref, o_ref, h_scr):
    @pl.when(pl.program_id(1) == 0)
    def _():
        h_scr[...] = _modulated_norm(x_ref, g_ref, sc_ref, sh_ref)

    o_ref[...] = jnp.dot(h_scr[...], w_ref[...], preferred_element_type=F32)


def _inproj_gates_kernel(x_ref, g_ref, sc_ref, sh_ref, w_ref, wg_ref, o_ref, og_ref, h_scr):
    @pl.when(pl.program_id(1) == 0)
    def _():
        h = _modulated_norm(x_ref, g_ref, sc_ref, sh_ref)
        h_scr[...] = h
        og_ref[...] = jnp.dot(h, wg_ref[...], preferred_element_type=F32)

    o_ref[...] = jnp.dot(h_scr[...], w_ref[...], preferred_element_type=F32)


def _input_projection(x2, seq, gain, scale, shift, w, w_gates=None, *, tm, tn):
    n, d = x2.shape
    n_out = w.shape[1]
    per_seq = seq // tm
    row = lambda i, j: (i, 0)
    mod = lambda i, j: (i // per_seq, 0, 0)
    in_specs = [
        pl.BlockSpec((tm, d), row),
        pl.BlockSpec((1, d), lambda i, j: (0, 0)),
        pl.BlockSpec((1, 1, d), mod),
        pl.BlockSpec((1, 1, d), mod),
        pl.BlockSpec((d, tn), lambda i, j: (0, j)),
    ]
    args = [x2, gain.reshape(1, d), scale, shift, w]
    out_specs = pl.BlockSpec((tm, tn), lambda i, j: (i, j))
    out_shape = jax.ShapeDtypeStruct((n, n_out), F32)
    body = _inproj_kernel
    if w_gates is not None:
        in_specs.append(pl.BlockSpec((d, LANES), lambda i, j: (0, 0)))
        args.append(w_gates)
        out_specs = (out_specs, pl.BlockSpec((tm, LANES), row))
        out_shape = (out_shape, jax.ShapeDtypeStruct((n, LANES), F32))
        body = _inproj_gates_kernel
    return pl.pallas_call(
        body,
        grid=(n // tm, n_out // tn),
        in_specs=in_specs,
        out_specs=out_specs,
        out_shape=out_shape,
        scratch_shapes=[pltpu.VMEM((tm, d), BF16)],
        compiler_params=_params("arbitrary", "arbitrary"),
        name="input_projection",
    )(*args)


def _stack_heads(rows, scale):
    parts = [rows[:, g * HEAD_DIM:(g + 1) * HEAD_DIM] for g in range(GQA_GROUP)]
    return jnp.concatenate(parts, axis=0) * scale


def _swa_kernel(sinks_ref, q_ref, kc_ref, vc_ref, kp_ref, vp_ref, o_ref, *, slopes):
    first_tile = pl.program_id(1) == 0
    w = SWA_WINDOW
    g_rows = GQA_GROUP * w
    kv_heads = kc_ref.shape[1] // HEAD_DIM
    row = lax.broadcasted_iota(jnp.int32, (g_rows, 2 * w), 0)
    ki = lax.broadcasted_iota(jnp.int32, (g_rows, 2 * w), 1)
    dist = (row & (w - 1)) + w - ki
    band = (dist >= 0) & (dist < w)
    first_key = jnp.where(first_tile, w, 0)
    distf = dist.astype(F32)
    head_of_row = row // w
    head_col = lax.broadcasted_iota(jnp.int32, (g_rows, 1), 0) // w
    for kv in range(kv_heads):
        slope = jnp.zeros((g_rows, 2 * w), F32)
        sink = jnp.zeros((g_rows, 1), F32)
        for g in range(GQA_GROUP):
            h = kv * GQA_GROUP + g
            slope = jnp.where(head_of_row == g, slopes[h], slope)
            sink = jnp.where(head_col == g, sinks_ref[h], sink)
        bias = -slope * distf
        lanes = slice(kv * HEAD_DIM, (kv + 1) * HEAD_DIM)
        for sb in range(q_ref.shape[0] // w):
            rows = slice(sb * w, (sb + 1) * w)
            q = _stack_heads(q_ref[rows, kv * GQA_GROUP * HEAD_DIM:(kv + 1) * GQA_GROUP * HEAD_DIM],
                             HEAD_DIM ** -0.5).astype(BF16)
            if sb == 0:
                k2 = jnp.concatenate([kp_ref[:, lanes], kc_ref[0:w, lanes]], axis=0)
                v2 = jnp.concatenate([vp_ref[:, lanes], vc_ref[0:w, lanes]], axis=0)
                mask = band & (ki >= first_key)
            else:
                k2 = kc_ref[(sb - 1) * w:(sb + 1) * w, lanes]
                v2 = vc_ref[(sb - 1) * w:(sb + 1) * w, lanes]
                mask = band
            s = lax.dot_general(q, k2.astype(BF16), NT, preferred_element_type=F32)
            logits = jnp.where(mask, s + bias, NEG_INF)
            m = jnp.maximum(jnp.max(logits, axis=-1, keepdims=True), sink)
            e = jnp.exp(logits - m)
            denom = jnp.sum(e, axis=-1, keepdims=True) + jnp.exp(sink - m)
            o = jnp.dot(e.astype(BF16), v2.astype(BF16), preferred_element_type=F32) / denom
            for g in range(GQA_GROUP):
                h = kv * GQA_GROUP + g
                o_ref[rows, h * HEAD_DIM:(h + 1) * HEAD_DIM] = o[g * w:(g + 1) * w]


def _sliding_window_attention(proj, sinks, batch, seq, *, width, k_col, v_col, tq):
    n = proj.shape[0]
    kvw = width // GQA_GROUP
    per_seq = seq // tq
    sub = tq // SWA_WINDOW
    cur = lambda col: (lambda b, i: (b * per_seq + i, col))
    prev = lambda col: (lambda b, i: (jnp.maximum((b * per_seq + i) * sub - 1, 0), col))
    return pl.pallas_call(
        functools.partial(_swa_kernel, slopes=_alibi_slopes(width // HEAD_DIM)),
        grid=(batch, per_seq),
        in_specs=[
            pl.BlockSpec(memory_space=pltpu.SMEM),
            pl.BlockSpec((tq, width), cur(0)),
            pl.BlockSpec((tq, kvw), cur(k_col)),
            pl.BlockSpec((tq, kvw), cur(v_col)),
            pl.BlockSpec((SWA_WINDOW, kvw), prev(k_col)),
            pl.BlockSpec((SWA_WINDOW, kvw), prev(v_col)),
        ],
        out_specs=pl.BlockSpec((tq, width), cur(0)),
        out_shape=jax.ShapeDtypeStruct((n, width), F32),
        compiler_params=_params("arbitrary", "arbitrary"),
        name="sliding_window_attention",
    )(sinks, proj, proj, proj, proj, proj)


def _pool_kernel(p_ref, halo_ref, w_ref, b_ref, sc_ref, o_ref):
    tt = p_ref.shape[0]
    gw = w_ref.shape[1]
    i = pl.program_id(1)
    pos1 = (i * tt + 1 + lax.broadcasted_iota(jnp.int32, (tt, 1), 0)).astype(F32)
    for g, win in enumerate(POOL_WINDOWS):
        lanes = slice(g * gw, (g + 1) * gw)
        cur = p_ref[:, lanes]
        halo = jnp.where(i > 0, halo_ref[:, lanes], 0.0)
        a = jnp.concatenate([halo, cur], axis=0)
        step = 1
        while step < win:
            a = a[step:] + a[:-step]
            step *= 2
        off = POOL_HALO - (win - 1)
        mean = a[off:off + tt] / jnp.minimum(pos1, float(win))
        d = (mean - cur).astype(BF16)
        y = jnp.dot(d, w_ref[g], preferred_element_type=F32) + b_ref[g]
        o_ref[:, lanes] = y * sc_ref[:, lanes]


def _pool_mixer(proj, pool_w, pool_b, pool_scale, batch, seq, *, col, tt):
    n = proj.shape[0]
    groups, gw, _ = pool_w.shape
    width = groups * gw
    per_seq = seq // tt
    halo_blocks = tt // POOL_HALO
    return pl.pallas_call(
        _pool_kernel,
        grid=(batch, per_seq),
        in_specs=[
            pl.BlockSpec((tt, width), lambda b, i: (b * per_seq + i, col)),
            pl.BlockSpec((POOL_HALO, width),
                         lambda b, i: (jnp.maximum((b * per_seq + i) * halo_blocks - 1, 0), col)),
            pl.BlockSpec((groups, gw, gw), lambda b, i: (0, 0, 0)),
            pl.BlockSpec((groups, 1, gw), lambda b, i: (0, 0, 0)),
            pl.BlockSpec((1, width), lambda b, i: (0, 0)),
        ],
        out_specs=pl.BlockSpec((tt, width), lambda b, i: (b * per_seq + i, 0)),
        out_shape=jax.ShapeDtypeStruct((n, width), F32),
        compiler_params=_params("arbitrary", "arbitrary"),
        name="pool_mixer",
    )(proj, proj, pool_w, pool_b.reshape(groups, 1, gw), pool_scale.reshape(1, width))


def _outproj_kernel(a_ref, b_ref, w_ref, x_ref, gate_ref, g_ref, o_ref):
    half = a_ref.shape[1]
    y = jnp.dot(a_ref[...].astype(BF16), w_ref[0:half, :], preferred_element_type=F32)
    y = y + jnp.dot(b_ref[...].astype(BF16), w_ref[half:, :], preferred_element_type=F32)
    o_ref[...] = x_ref[...] + gate_ref[0] * _rms(y, g_ref[...])


def _output_projection(a, b, w, x2, seq, gate, gain, *, tm):
    n, d = x2.shape
    half = a.shape[1]
    per_seq = seq // tm
    row = lambda i: (i, 0)
    return pl.pallas_call(
        _outproj_kernel,
        grid=(n // tm,),
        in_specs=[
            pl.BlockSpec((tm, half), row),
            pl.BlockSpec((tm, half), row),
            pl.BlockSpec((2 * half, d), lambda i: (0, 0)),
            pl.BlockSpec((tm, d), row),
            pl.BlockSpec((1, 1, d), lambda i: (i // per_seq, 0, 0)),
            pl.BlockSpec((1, d), lambda i: (0, 0)),
        ],
        out_specs=pl.BlockSpec((tm, d), row),
        out_shape=jax.ShapeDtypeStruct((n, d), F32),
        compiler_params=_params("arbitrary"),
        name="output_projection",
    )(a, b, w, x2, gate, gain.reshape(1, d))


def _ffn_kernel(x_ref, g_ref, sc_ref, sh_ref, wug_ref, cw_ref, cb_ref, wd_ref,
                gate_ref, gpost_ref, o_ref, h_scr, act_scr, carry_scr, *, per_seq):
    i = pl.program_id(0)
    j = pl.program_id(1)
    tm, d = x_ref.shape
    tf = wd_ref.shape[0]

    @pl.when(j == 0)
    def _():
        h_scr[...] = _modulated_norm(x_ref, g_ref, sc_ref, sh_ref)
        o_ref[...] = jnp.zeros_like(o_ref)

    @pl.when((j == 0) & (i % per_seq == 0))
    def _():
        carry_scr[...] = jnp.zeros_like(carry_scr)

    h = h_scr[...]
    row = lax.broadcasted_iota(jnp.int32, (tm, LANES), 0)
    for k in range(tf // LANES):
        cols = slice(k * LANES, (k + 1) * LANES)
        ug = jnp.dot(h, wug_ref[:, 2 * k * LANES:2 * (k + 1) * LANES], preferred_element_type=F32)
        u = ug[:, :LANES]
        g = ug[:, LANES:]
        tail = carry_scr[j, :, cols]
        carry_scr[j, :, cols] = g[tm - SUBLANES:, :]
        g1 = jnp.where(row == 0, tail[SUBLANES - 1:, :], pltpu.roll(g, 1, 0))
        g2 = jnp.where(row == 0, tail[SUBLANES - 2:SUBLANES - 1, :],
                       jnp.where(row == 1, tail[SUBLANES - 1:, :], pltpu.roll(g, 2, 0)))
        cw = cw_ref[:, cols]
        gc = cb_ref[:, cols] + cw[0:1] * g2 + cw[1:2] * g1 + cw[2:3] * g
        act = 0.5 * gc * (1.0 + jnp.tanh(math.sqrt(2.0 / math.pi) * (gc + 0.044715 * (gc * gc * gc))))
        act_scr[:, cols] = (act * u).astype(BF16)

    act = act_scr[...]
    for n0 in range(0, d, d // 2):
        out_cols = slice(n0, n0 + d // 2)
        o_ref[:, out_cols] += jnp.dot(act, wd_ref[:, out_cols], preferred_element_type=F32)

    @pl.when(j == pl.num_programs(1) - 1)
    def _():
        o_ref[...] = x_ref[...] + gate_ref[0] * _rms(o_ref[...], gpost_ref[...])


def _conv_ffn(x2, seq, gain, scale, shift, w_up, conv_w, conv_b, w_down, gate, gain_post, *, tm, tf):
    n, d = x2.shape
    d_ff = w_down.shape[0]
    nf = d_ff // tf
    per_seq = seq // tm
    row = lambda i, j: (i, 0)
    mod = lambda i, j: (i // per_seq, 0, 0)
    const = lambda i, j: (0, 0)
    return pl.pallas_call(
        functools.partial(_ffn_kernel, per_seq=per_seq),
        grid=(n // tm, nf),
        in_specs=[
            pl.BlockSpec((tm, d), row),
            pl.BlockSpec((1, d), const),
            pl.BlockSpec((1, 1, d), mod),
            pl.BlockSpec((1, 1, d), mod),
            pl.BlockSpec((d, 2 * tf), lambda i, j: (0, j)),
            pl.BlockSpec((FFN_CONV, tf), lambda i, j: (0, j)),
            pl.BlockSpec((1, tf), lambda i, j: (0, j)),
            pl.BlockSpec((tf, d), lambda i, j: (j, 0)),
            pl.BlockSpec((1, 1, d), mod),
            pl.BlockSpec((1, d), const),
        ],
        out_specs=pl.BlockSpec((tm, d), row),
        out_shape=jax.ShapeDtypeStruct((n, d), F32),
        scratch_shapes=[
            pltpu.VMEM((tm, d), BF16),
            pltpu.VMEM((tm, tf), BF16),
            pltpu.VMEM((nf, SUBLANES, tf), F32),
        ],
        compiler_params=_params("arbitrary", "arbitrary"),
        name="conv_ffn",
    )(x2, gain.reshape(1, d), scale, shift, w_up, conv_w, conv_b.reshape(1, d_ff), w_down,
      gate, gain_post.reshape(1, d))


def _interleave_up_projection(w_up):
    d, two_ff = w_up.shape
    groups = two_ff // 2 // LANES
    return w_up.reshape(d, 2, groups, LANES).swapaxes(1, 2).reshape(d, two_ff).astype(BF16)


def _mlstm_kernel(qk_ref, v_ref, o_ref, gt_ref, cw_ref, cb_ref, gb_ref, mh_ref, out_ref,
                  tail_scr, c_scr, n_scr, m_scr):
    L = qk_ref.shape[0]
    heads = MLSTM_HEADS
    dqk = qk_ref.shape[1] // (2 * heads)
    dv = v_ref.shape[1] // heads

    @pl.when(pl.program_id(1) == 0)
    def _():
        tail_scr[...] = jnp.zeros_like(tail_scr)
        c_scr[...] = jnp.zeros_like(c_scr)
        n_scr[...] = jnp.zeros_like(n_scr)
        m_scr[...] = jnp.zeros_like(m_scr)

    raw = qk_ref[...]
    tail = tail_scr[...]
    tail_scr[...] = raw[L - SUBLANES:, :]
    row = lax.broadcasted_iota(jnp.int32, raw.shape, 0)
    cw = cw_ref[...]
    y = cb_ref[...] + cw[MLSTM_CONV - 1:MLSTM_CONV] * raw
    for lag in range(1, MLSTM_CONV):
        shifted = pltpu.roll(raw, lag, 0)
        for r in range(lag):
            shifted = jnp.where(row == r, tail[SUBLANES - lag + r:SUBLANES - lag + r + 1, :], shifted)
        y = y + cw[MLSTM_CONV - 1 - lag:MLSTM_CONV - lag] * shifted
    qk = y * _sigmoid(y)
    q_all = qk[:, :heads * dqk]
    k_all = qk[:, heads * dqk:] * (dqk ** -0.5)

    gates = gt_ref[...] + gb_ref[...]
    lf = _log_sigmoid(gates)
    r_i = lax.broadcasted_iota(jnp.int32, (L, L), 0)
    c_i = lax.broadcasted_iota(jnp.int32, (L, L), 1)
    causal = r_i >= c_i
    tri = causal.astype(F32)
    b_cols = jnp.dot(tri, lf, preferred_element_type=F32, precision=lax.Precision.HIGHEST)
    gates_t = gates.T
    lf_t = _log_sigmoid(gates_t)
    b_rows = lax.dot_general(lf_t, tri, NT, preferred_element_type=F32, precision=lax.Precision.HIGHEST)

    for h in range(heads):
        i_col = gates[:, h:h + 1]
        i_row = gates_t[h:h + 1, :]
        b_col = b_cols[:, heads + h:heads + h + 1]
        b_row = b_rows[heads + h:heads + h + 1, :]
        m_prev = m_scr[h][:, 0:1]
        n_prev = n_scr[h]
        c_prev = c_scr[h]
        qh = q_all[:, h * dqk:(h + 1) * dqk]
        kh = k_all[:, h * dqk:(h + 1) * dqk]
        vh = v_ref[:, h * dv:(h + 1) * dv].astype(BF16)
        qb = qh.astype(BF16)

        logd = jnp.where(causal, b_col - b_row + i_row, -jnp.inf)
        m_inter = m_prev + b_col
        m_row = jnp.maximum(m_inter, jnp.max(logd, axis=-1, keepdims=True))
        w_inter = jnp.exp(m_inter - m_row)
        s = lax.dot_general(qb, kh.astype(BF16), NT, preferred_element_type=F32) * jnp.exp(logd - m_row)
        num = (w_inter * jnp.dot(qb, c_prev.astype(BF16), preferred_element_type=F32)
               + jnp.dot(s.astype(BF16), vh, preferred_element_type=F32))
        den = w_inter * jnp.sum(qh * n_prev, axis=-1, keepdims=True) + jnp.sum(s, axis=-1, keepdims=True)
        hh = num / jnp.maximum(jnp.abs(den), jnp.exp(-m_row))

        b_last = b_col[L - 1:, :]
        logw = b_last - b_col + i_col
        m_new = jnp.maximum(m_prev + b_last, jnp.max(logw, axis=0, keepdims=True))
        w_old = jnp.exp(m_prev + b_last - m_new)
        wk = jnp.exp(logw - m_new) * kh
        c_scr[h] = w_old * c_prev + jnp.dot(wk.T.astype(BF16), vh, preferred_element_type=F32)
        n_scr[h] = w_old * n_prev + jnp.sum(wk, axis=0, keepdims=True)
        m_scr[h] = jnp.broadcast_to(m_new, (1, LANES))

        cols = slice(h * dv, (h + 1) * dv)
        out_ref[:, cols] = _rms(hh, mh_ref[:, cols]) * _sigmoid(o_ref[:, cols])


def _mlstm(proj, gates, conv_w, conv_b, gate_b, mh_norm, batch, seq):
    n = proj.shape[0]
    width = mh_norm.shape[0]
    heads = MLSTM_HEADS
    dqk = width // heads // 2
    L = MLSTM_CHUNK
    per_seq = seq // L
    blk = lambda col: (lambda b, c: (b * per_seq + c, col))
    const = lambda b, c: (0, 0)
    gate_b_pad = jnp.zeros((1, LANES), F32).at[0, :2 * heads].set(gate_b)
    return pl.pallas_call(
        _mlstm_kernel,
        grid=(batch, per_seq),
        in_specs=[
            pl.BlockSpec((L, width), blk(0)),
            pl.BlockSpec((L, width), blk(1)),
            pl.BlockSpec((L, width), blk(2)),
            pl.BlockSpec((L, LANES), blk(0)),
            pl.BlockSpec((MLSTM_CONV, width), const),
            pl.BlockSpec((1, width), const),
            pl.BlockSpec((1, LANES), const),
            pl.BlockSpec((1, width), const),
        ],
        out_specs=pl.BlockSpec((L, width), blk(0)),
        out_shape=jax.ShapeDtypeStruct((n, width), F32),
        scratch_shapes=[
            pltpu.VMEM((SUBLANES, width), F32),
            pltpu.VMEM((heads, dqk, width // heads), F32),
            pltpu.VMEM((heads, 1, dqk), F32),
            pltpu.VMEM((heads, 1, LANES), F32),
        ],
        compiler_params=_params("arbitrary", "arbitrary"),
        name="mlstm",
    )(proj, proj, proj, gates, conv_w, conv_b.reshape(1, width), gate_b_pad, mh_norm.reshape(1, width))


MOBA_SPLIT = 3
MOBA_EXTRA = HEAD_DIM


def _split_bf16(x):
    terms = []
    for _ in range(MOBA_SPLIT - 1):
        t = x.astype(BF16).astype(F32)
        terms.append(t)
        x = x - t
    terms.append(x)
    return terms


def _moba_prepare_kernel(k_ref, v_ref, km_ref, ka_ref, vt_ref):
    nblk = km_ref.shape[1]
    bs = MOBA_BLOCK
    seq, kvw = k_ref.shape
    k = k_ref[...]
    km_ref[0] = jnp.mean(k.reshape(nblk, bs, kvw), axis=1)
    col = lax.broadcasted_iota(jnp.int32, (seq, MOBA_EXTRA), 1)
    row = lax.broadcasted_iota(jnp.int32, (seq, MOBA_EXTRA), 0)
    in_onehot = col < MOBA_SPLIT * nblk
    in_offset = (col >= MOBA_SPLIT * nblk) & (col < MOBA_SPLIT * (nblk + 1))
    onehot = jnp.where(in_onehot & (col % nblk == row // bs), 1.0, 0.0)
    extra = jnp.where(in_offset, (row % bs).astype(F32), onehot)
    for kv in range(kvw // HEAD_DIM):
        ka_ref[0, kv] = jnp.concatenate([k[:, kv * HEAD_DIM:(kv + 1) * HEAD_DIM], extra], axis=1).astype(BF16)
    for j in range(nblk):
        vt_ref[0, j] = v_ref[j * bs:(j + 1) * bs, :].T.astype(BF16)


def _moba_prepare(proj, batch, seq, *, k_col, v_col, kvw):
    nblk = seq // MOBA_BLOCK
    kv_heads = kvw // HEAD_DIM
    return pl.pallas_call(
        _moba_prepare_kernel,
        grid=(batch,),
        in_specs=[pl.BlockSpec((seq, kvw), lambda b: (b, k_col)),
                  pl.BlockSpec((seq, kvw), lambda b: (b, v_col))],
        out_specs=(pl.BlockSpec((1, nblk, kvw), lambda b: (b, 0, 0)),
                   pl.BlockSpec((1, kv_heads, seq, HEAD_DIM + MOBA_EXTRA), lambda b: (b, 0, 0, 0)),
                   pl.BlockSpec((1, nblk, kvw, MOBA_BLOCK), lambda b: (b, 0, 0, 0))),
        out_shape=(jax.ShapeDtypeStruct((batch, nblk, kvw), F32),
                   jax.ShapeDtypeStruct((batch, kv_heads, seq, HEAD_DIM + MOBA_EXTRA), BF16),
                   jax.ShapeDtypeStruct((batch, nblk, kvw, MOBA_BLOCK), BF16)),
        compiler_params=_params("arbitrary"),
        name="moba_prepare",
    )(proj, proj)


def _moba_kernel(q_ref, ka_ref, vt_ref, km_ref, o_ref, qa_scr, acc_scr, m_scr, l_scr, ot_scr, *, slopes):
    n = pl.program_id(1)
    bs = MOBA_BLOCK
    nblk = km_ref.shape[1]
    kv_heads = ka_ref.shape[1]
    cols = GQA_GROUP * bs

    lane = lax.broadcasted_iota(jnp.int32, (1, cols), 1)
    head_of_lane = lane // bs
    blk_t = lax.broadcasted_iota(jnp.int32, (nblk, cols), 0)
    dist0 = ((n - blk_t) * bs + (lane & (bs - 1))).astype(F32)
    sub = lax.broadcasted_iota(jnp.int32, (SUBLANES, cols), 0)
    pad = jnp.zeros((MOBA_EXTRA - MOBA_SPLIT * nblk - SUBLANES, cols), F32)

    q_t = q_ref[...].T

    for kv in range(kv_heads):
        slope = jnp.zeros((1, cols), F32)
        for g in range(GQA_GROUP):
            slope = jnp.where(head_of_lane == g, slopes[kv * GQA_GROUP + g], slope)
        heads = [q_t[(kv * GQA_GROUP + g) * HEAD_DIM:(kv * GQA_GROUP + g + 1) * HEAD_DIM, :]
                 for g in range(GQA_GROUP)]
        qf_t = jnp.concatenate(heads, axis=1) * HEAD_DIM ** -0.5

        gate_t = jnp.dot(km_ref[0][:, kv * HEAD_DIM:(kv + 1) * HEAD_DIM], qf_t,
                         preferred_element_type=F32, precision=lax.Precision.HIGHEST)
        rank = jnp.zeros((nblk, cols), jnp.int32)
        for c in range(nblk):
            gc = gate_t[c:c + 1, :]
            beats = (gc > gate_t) | ((gc == gate_t) & (c < blk_t))
            rank = rank + jnp.where(beats, jnp.where(c < n, 1, 0), 0)
        chosen = ((blk_t < n) & (rank < MOBA_TOPK)) | (blk_t == n)
        bias_t = jnp.where(chosen, 0.0, NEG_INF) - slope * dist0
        slope_rows = jnp.zeros((SUBLANES, cols), F32)
        for t, term in enumerate(_split_bf16(slope)):
            slope_rows = jnp.where(sub == t, term, slope_rows)
        qa_scr[kv] = jnp.concatenate([qf_t] + _split_bf16(bias_t) + [slope_rows, pad], axis=0).astype(BF16)

    m_scr[...] = jnp.full_like(m_scr, NEG_INF)
    l_scr[...] = jnp.zeros_like(l_scr)
    acc_scr[...] = jnp.zeros_like(acc_scr)

    def attend(j0, nb, mask):
        start = pl.multiple_of(j0 * bs, bs)
        for kv in range(kv_heads):
            s = jnp.dot(ka_ref[0, kv, pl.ds(start, nb * bs), :], qa_scr[kv], preferred_element_type=F32)
            if mask is not None:
                s = jnp.where(mask, s, NEG_INF)
            m_old = m_scr[kv]
            m_new = jnp.maximum(m_old, jnp.max(s, axis=0, keepdims=True))
            alpha = jnp.exp(m_old - m_new)
            p = jnp.exp(s - m_new)
            v_t = jnp.concatenate([vt_ref[0, j0 + t, kv * HEAD_DIM:(kv + 1) * HEAD_DIM, :] for t in range(nb)],
                                  axis=1)
            acc_scr[kv] = alpha * acc_scr[kv] + jnp.dot(v_t, p.astype(BF16), preferred_element_type=F32)
            l_scr[kv] = alpha * l_scr[kv] + jnp.sum(p, axis=0, keepdims=True)
            m_scr[kv] = m_new

    tok = lax.broadcasted_iota(jnp.int32, (1, cols), 1) & (bs - 1)
    odd = (n + 1) % 2

    @pl.when(odd == 1)
    def _():
        visible = jnp.where(n > 0, bs, 0)
        key1 = lax.broadcasted_iota(jnp.int32, (bs, cols), 0)
        attend(0, 1, tok + visible >= key1)

    def body(i, carry):
        attend(odd + 2 * i, 2, None)
        return carry

    lax.fori_loop(0, (n + 1) // 2 - 1, body, 0)

    @pl.when(n >= 1)
    def _():
        key2 = lax.broadcasted_iota(jnp.int32, (2 * bs, cols), 0)
        attend(n - 1, 2, tok + bs >= key2)

    for kv in range(kv_heads):
        o_t = acc_scr[kv] / l_scr[kv]
        for g in range(GQA_GROUP):
            h = kv * GQA_GROUP + g
            ot_scr[h * HEAD_DIM:(h + 1) * HEAD_DIM, :] = o_t[:, g * bs:(g + 1) * bs]
    o_ref[...] = ot_scr[...].T


def _moba_attention(proj, batch, seq, *, width, q_col, k_col, v_col):
    n = proj.shape[0]
    kvw = width // GQA_GROUP
    kv_heads = kvw // HEAD_DIM
    bs = MOBA_BLOCK
    nblk = seq // bs
    assert MOBA_SPLIT * nblk + SUBLANES <= MOBA_EXTRA
    k_means, k_aug, v_t = _moba_prepare(proj, batch, seq, k_col=k_col, v_col=v_col, kvw=kvw)
    cols = GQA_GROUP * bs
    return pl.pallas_call(
        functools.partial(_moba_kernel, slopes=_alibi_slopes(width // HEAD_DIM)),
        grid=(batch, nblk),
        in_specs=[
            pl.BlockSpec((bs, width), lambda b, i: (b * nblk + i, q_col)),
            pl.BlockSpec((1, kv_heads, seq, HEAD_DIM + MOBA_EXTRA), lambda b, i: (b, 0, 0, 0)),
            pl.BlockSpec((1, nblk, kvw, bs), lambda b, i: (b, 0, 0, 0)),
            pl.BlockSpec((1, nblk, kvw), lambda b, i: (b, 0, 0)),
        ],
        out_specs=pl.BlockSpec((bs, width), lambda b, i: (b * nblk + i, 0)),
        out_shape=jax.ShapeDtypeStruct((n, width), F32),
        scratch_shapes=[
            pltpu.VMEM((kv_heads, HEAD_DIM + MOBA_EXTRA, cols), BF16),
            pltpu.VMEM((kv_heads, HEAD_DIM, cols), F32),
            pltpu.VMEM((kv_heads, 1, cols), F32),
            pltpu.VMEM((kv_heads, 1, cols), F32),
            pltpu.VMEM((width, bs), F32),
        ],
        compiler_params=_params("arbitrary", "arbitrary"),
        name="moba_attention",
    )(proj, k_aug, v_t, k_means)


def kernel(x, c, ada_w, ada_b, norm_mix_pre, norm_mix_post, norm_ffn_pre, norm_ffn_post, ffn_w_up, ffn_conv_w, ffn_conv_b, ffn_w_down, ev_w_in, ev_w_out, ev_sinks, ev_pool_w, ev_pool_b, ev_pool_scale, od_w_in, od_w_out, od_conv_w, od_conv_b, od_gate_b, od_mh_norm):
    batch, seq, d = x.shape
    depth = ada_w.shape[0]
    assert seq % 512 == 0
    x2 = x.reshape(batch * seq, d)

    mod = _ada_modulation(c, ada_w, ada_b)
    mod = mod.reshape(depth, batch, 6, 1, d)

    for layer in range(depth):
        sh_m, sc_m, gt_m, sh_f, sc_f, gt_f = [mod[layer, :, r] for r in range(6)]
        j = layer // 2
        if layer % 2 == 0:
            width = ev_pool_scale.shape[1]
            kvw = width // GQA_GROUP
            w_in = ev_w_in[j]
            w_in = jnp.concatenate([w_in[:, :width], w_in[:, width + 2 * kvw:],
                                    w_in[:, width:width + 2 * kvw]], axis=1).astype(BF16)
            proj = _input_projection(x2, seq, norm_mix_pre[layer], sc_m, sh_m, w_in, tm=1024, tn=1280)
            a = _sliding_window_attention(proj, ev_sinks[j], batch, seq, width=width,
                                          k_col=2 * width // kvw, v_col=2 * width // kvw + 1, tq=512)
            b = _pool_mixer(proj, ev_pool_w[j].astype(BF16), ev_pool_b[j], ev_pool_scale[j], batch, seq,
                            col=1, tt=512)
            w_out = ev_w_out[j].astype(BF16)
        else:
            width = od_mh_norm.shape[1]
            kvw = width // GQA_GROUP
            heads = MLSTM_HEADS
            w_in = od_w_in[j]
            g0 = 3 * width
            w_main = jnp.concatenate([w_in[:, :g0], w_in[:, g0 + 2 * heads:]], axis=1).astype(BF16)
            w_gates = jnp.zeros((d, LANES), BF16).at[:, :2 * heads].set(
                w_in[:, g0:g0 + 2 * heads].astype(BF16))
            proj, gates = _input_projection(x2, seq, norm_mix_pre[layer], sc_m, sh_m, w_main, w_gates,
                                            tm=1024, tn=1536)
            a = _mlstm(proj, gates, od_conv_w[j], od_conv_b[j], od_gate_b[j], od_mh_norm[j], batch, seq)
            b = _moba_attention(proj, batch, seq, width=width, q_col=3,
                                k_col=4 * width // kvw, v_col=4 * width // kvw + 1)
            w_out = od_w_out[j].astype(BF16)
        x2 = _output_projection(a, b, w_out, x2, seq, gt_m, norm_mix_post[layer], tm=512)
        x2 = _conv_ffn(x2, seq, norm_ffn_pre[layer], sc_f, sh_f, _interleave_up_projection(ffn_w_up[layer]),
                       ffn_conv_w[layer], ffn_conv_b[layer], ffn_w_down[layer].astype(BF16),
                       gt_f, norm_ffn_post[layer], tm=512, tf=512)
    return x2.reshape(batch, seq, d)
```

```python
import functools
import math

import jax
import jax.numpy as jnp
from jax import lax
from jax.experimental import pallas as pl
from jax.experimental.pallas import tpu as pltpu

F32 = jnp.float32
BF16 = jnp.bfloat16

HEAD_DIM = 64
GQA_GROUP = 4
SWA_WINDOW = 128
POOL_WINDOWS = (2, 4, 8, 16)
POOL_HALO = 16
MLSTM_HEADS = 4
MLSTM_CONV = 4
MLSTM_CHUNK = 256
MOBA_BLOCK = 256
MOBA_TOPK = 3
FFN_CONV = 3
FFN_CHUNK = 256
NORM_EPS = 1e-6
NEG_INF = -1e30
LANES = 128
SUBLANES = 8
VMEM_LIMIT = 60 * 1024 * 1024

NT = (((1,), (1,)), ((), ()))


def _alibi_slopes(n_heads):
    return [2.0 ** (-8.0 * (h + 1) / n_heads) for h in range(n_heads)]


def _params(*semantics):
    return pltpu.CompilerParams(dimension_semantics=semantics, vmem_limit_bytes=VMEM_LIMIT)


def _rms(x, gain):
    ms = jnp.mean(x * x, axis=-1, keepdims=True)
    return x * lax.rsqrt(ms + NORM_EPS) * gain


def _sigmoid(x):
    return 1.0 / (1.0 + jnp.exp(-x))


def _log_sigmoid(x):
    return jnp.minimum(x, 0.0) - jnp.log(1.0 + jnp.exp(-jnp.abs(x)))


def _ada_kernel(c_ref, w_ref, b_ref, o_ref):
    c = c_ref[...]
    a = (c * _sigmoid(c)).astype(BF16)
    o_ref[0] = jnp.dot(a, w_ref[0].astype(BF16), preferred_element_type=F32) + b_ref[0]


def _ada_modulation(c, ada_w, ada_b):
    depth, d, n_out = ada_w.shape
    rows = 2 * SUBLANES
    c_pad = jnp.zeros((rows, d), F32).at[:c.shape[0]].set(c)
    tn = 1024
    out = pl.pallas_call(
        _ada_kernel,
        grid=(depth, n_out // tn),
        in_specs=[
            pl.BlockSpec((rows, d), lambda l, j: (0, 0)),
            pl.BlockSpec((1, d, tn), lambda l, j: (l, 0, j)),
            pl.BlockSpec((1, 1, tn), lambda l, j: (l, 0, j)),
        ],
        out_specs=pl.BlockSpec((1, rows, tn), lambda l, j: (l, 0, j)),
        out_shape=jax.ShapeDtypeStruct((depth, rows, n_out), F32),
        compiler_params=_params("arbitrary", "arbitrary"),
        name="ada_modulation",
    )(c_pad, ada_w, ada_b.reshape(depth, 1, n_out))
    return out[:, :c.shape[0]]


def _modulated_norm(x_ref, g_ref, sc_ref, sh_ref):
    return (_rms(x_ref[...], g_ref[...]) * (1.0 + sc_ref[0]) + sh_ref[0]).astype(BF16)


def _inproj_kernel(x_ref, g_ref, sc_ref, sh_ref, w_ref, o_ref, h_scr):
    @pl.when(pl.program_id(1) == 0)
    def _():
        h_scr[...] = _modulated_norm(x_ref, g_ref, sc_ref, sh_ref)

    o_ref[...] = jnp.dot(h_scr[...], w_ref[...], preferred_element_type=F32)


def _inproj_gates_kernel(x_ref, g_ref, sc_ref, sh_ref, w_ref, wg_ref, o_ref, og_ref, h_scr):
    @pl.when(pl.program_id(1) == 0)
    def _():
        h = _modulated_norm(x_ref, g_ref, sc_ref, sh_ref)
        h_scr[...] = h
        og_ref[...] = jnp.dot(h, wg_ref[...], preferred_element_type=F32)

    o_ref[...] = jnp.dot(h_scr[...], w_ref[...], preferred_element_type=F32)


def _input_projection(x2, seq, gain, scale, shift, w, w_gates=None, *, tm, tn):
    n, d = x2.shape
    n_out = w.shape[1]
    per_seq = seq // tm
    row = lambda i, j: (i, 0)
    mod = lambda i, j: (i // per_seq, 0, 0)
    in_specs = [
        pl.BlockSpec((tm, d), row),
        pl.BlockSpec((1, d), lambda i, j: (0, 0)),
        pl.BlockSpec((1, 1, d), mod),
        pl.BlockSpec((1, 1, d), mod),
        pl.BlockSpec((d, tn), lambda i, j: (0, j)),
    ]
    args = [x2, gain.reshape(1, d), scale, shift, w]
    out_specs = pl.BlockSpec((tm, tn), lambda i, j: (i, j))
    out_shape = jax.ShapeDtypeStruct((n, n_out), F32)
    body = _inproj_kernel
    if w_gates is not None:
        in_specs.append(pl.BlockSpec((d, LANES), lambda i, j: (0, 0)))
        args.append(w_gates)
        out_specs = (out_specs, pl.BlockSpec((tm, LANES), row))
        out_shape = (out_shape, jax.ShapeDtypeStruct((n, LANES), F32))
        body = _inproj_gates_kernel
    return pl.pallas_call(
        body,
        grid=(n // tm, n_out // tn),
        in_specs=in_specs,
        out_specs=out_specs,
        out_shape=out_shape,
        scratch_shapes=[pltpu.VMEM((tm, d), BF16)],
        compiler_params=_params("arbitrary", "arbitrary"),
        name="input_projection",
    )(*args)


def _stack_heads(rows, scale):
    parts = [rows[:, g * HEAD_DIM:(g + 1) * HEAD_DIM] for g in range(GQA_GROUP)]
    return jnp.concatenate(parts, axis=0) * scale


def _swa_kernel(sinks_ref, q_ref, kc_ref, vc_ref, kp_ref, vp_ref, o_ref, *, slopes):
    first_tile = pl.program_id(1) == 0
    w = SWA_WINDOW
    g_rows = GQA_GROUP * w
    kv_heads = kc_ref.shape[1] // HEAD_DIM
    row = lax.broadcasted_iota(jnp.int32, (g_rows, 2 * w), 0)
    ki = lax.broadcasted_iota(jnp.int32, (g_rows, 2 * w), 1)
    dist = (row & (w - 1)) + w - ki
    band = (dist >= 0) & (dist < w)
    first_key = jnp.where(first_tile, w, 0)
    distf = dist.astype(F32)
    head_of_row = row // w
    head_col = lax.broadcasted_iota(jnp.int32, (g_rows, 1), 0) // w
    for kv in range(kv_heads):
        slope = jnp.zeros((g_rows, 2 * w), F32)
        sink = jnp.zeros((g_rows, 1), F32)
        for g in range(GQA_GROUP):
            h = kv * GQA_GROUP + g
            slope = jnp.where(head_of_row == g, slopes[h], slope)
            sink = jnp.where(head_col == g, sinks_ref[h], sink)
        bias = -slope * distf
        lanes = slice(kv * HEAD_DIM, (kv + 1) * HEAD_DIM)
        for sb in range(q_ref.shape[0] // w):
            rows = slice(sb * w, (sb + 1) * w)
            q = _stack_heads(q_ref[rows, kv * GQA_GROUP * HEAD_DIM:(kv + 1) * GQA_GROUP * HEAD_DIM],
                             HEAD_DIM ** -0.5).astype(BF16)
            if sb == 0:
                k2 = jnp.concatenate([kp_ref[:, lanes], kc_ref[0:w, lanes]], axis=0)
                v2 = jnp.concatenate([vp_ref[:, lanes], vc_ref[0:w, lanes]], axis=0)
                mask = band & (ki >= first_key)
            else:
                k2 = kc_ref[(sb - 1) * w:(sb + 1) * w, lanes]
                v2 = vc_ref[(sb - 1) * w:(sb + 1) * w, lanes]
                mask = band
            s = lax.dot_general(q, k2.astype(BF16), NT, preferred_element_type=F32)
            logits = jnp.where(mask, s + bias, NEG_INF)
            m = jnp.maximum(jnp.max(logits, axis=-1, keepdims=True), sink)
            e = jnp.exp(logits - m)
            denom = jnp.sum(e, axis=-1, keepdims=True) + jnp.exp(sink - m)
            o = jnp.dot(e.astype(BF16), v2.astype(BF16), preferred_element_type=F32) / denom
            for g in range(GQA_GROUP):
                h = kv * GQA_GROUP + g
                o_ref[rows, h * HEAD_DIM:(h + 1) * HEAD_DIM] = o[g * w:(g + 1) * w]


def _sliding_window_attention(proj, sinks, batch, seq, *, width, k_col, v_col, tq):
    n = proj.shape[0]
    kvw = width // GQA_GROUP
    per_seq = seq // tq
    sub = tq // SWA_WINDOW
    cur = lambda col: (lambda b, i: (b * per_seq + i, col))
    prev = lambda col: (lambda b, i: (jnp.maximum((b * per_seq + i) * sub - 1, 0), col))
    return pl.pallas_call(
        functools.partial(_swa_kernel, slopes=_alibi_slopes(width // HEAD_DIM)),
        grid=(batch, per_seq),
        in_specs=[
            pl.BlockSpec(memory_space=pltpu.SMEM),
            pl.BlockSpec((tq, width), cur(0)),
            pl.BlockSpec((tq, kvw), cur(k_col)),
            pl.BlockSpec((tq, kvw), cur(v_col)),
            pl.BlockSpec((SWA_WINDOW, kvw), prev(k_col)),
            pl.BlockSpec((SWA_WINDOW, kvw), prev(v_col)),
        ],
        out_specs=pl.BlockSpec((tq, width), cur(0)),
        out_shape=jax.ShapeDtypeStruct((n, width), F32),
        compiler_params=_params("arbitrary", "arbitrary"),
        name="sliding_window_attention",
    )(sinks, proj, proj, proj, proj, proj)


def _pool_kernel(p_ref, halo_ref, w_ref, b_ref, sc_ref, o_ref):
    tt = p_ref.shape[0]
    gw = w_ref.shape[1]
    i = pl.program_id(1)
    pos1 = (i * tt + 1 + lax.broadcasted_iota(jnp.int32, (tt, 1), 0)).astype(F32)
    for g, win in enumerate(POOL_WINDOWS):
        lanes = slice(g * gw, (g + 1) * gw)
        cur = p_ref[:, lanes]
        halo = jnp.where(i > 0, halo_ref[:, lanes], 0.0)
        a = jnp.concatenate([halo, cur], axis=0)
        step = 1
        while step < win:
            a = a[step:] + a[:-step]
            step *= 2
        off = POOL_HALO - (win - 1)
        mean = a[off:off + tt] / jnp.minimum(pos1, float(win))
        d = (mean - cur).astype(BF16)
        y = jnp.dot(d, w_ref[g], preferred_element_type=F32) + b_ref[g]
        o_ref[:, lanes] = y * sc_ref[:, lanes]


def _pool_mixer(proj, pool_w, pool_b, pool_scale, batch, seq, *, col, tt):
    n = proj.shape[0]
    groups, gw, _ = pool_w.shape
    width = groups * gw
    per_seq = seq // tt
    halo_blocks = tt // POOL_HALO
    return pl.pallas_call(
        _pool_kernel,
        grid=(batch, per_seq),
        in_specs=[
            pl.BlockSpec((tt, width), lambda b, i: (b * per_seq + i, col)),
            pl.BlockSpec((POOL_HALO, width),
                         lambda b, i: (jnp.maximum((b * per_seq + i) * halo_blocks - 1, 0), col)),
            pl.BlockSpec((groups, gw, gw), lambda b, i: (0, 0, 0)),
            pl.BlockSpec((groups, 1, gw), lambda b, i: (0, 0, 0)),
            pl.BlockSpec((1, width), lambda b, i: (0, 0)),
        ],
        out_specs=pl.BlockSpec((tt, width), lambda b, i: (b * per_seq + i, 0)),
        out_shape=jax.ShapeDtypeStruct((n, width), F32),
        compiler_params=_params("arbitrary", "arbitrary"),
        name="pool_mixer",
    )(proj, proj, pool_w, pool_b.reshape(groups, 1, gw), pool_scale.reshape(1, width))


def _outproj_kernel(a_ref, b_ref, w_ref, x_ref, gate_ref, g_ref, o_ref):
    half = a_ref.shape[1]
    y = jnp.dot(a_ref[...].astype(BF16), w_ref[0:half, :], preferred_element_type=F32)
    y = y + jnp.dot(b_ref[...].astype(BF16), w_ref[half:, :], preferred_element_type=F32)
    o_ref[...] = x_ref[...] + gate_ref[0] * _rms(y, g_ref[...])


def _output_projection(a, b, w, x2, seq, gate, gain, *, tm):
    n, d = x2.shape
    half = a.shape[1]
    per_seq = seq // tm
    row = lambda i: (i, 0)
    return pl.pallas_call(
        _outproj_kernel,
        grid=(n // tm,),
        in_specs=[
            pl.BlockSpec((tm, half), row),
            pl.BlockSpec((tm, half), row),
            pl.BlockSpec((2 * half, d), lambda i: (0, 0)),
            pl.BlockSpec((tm, d), row),
            pl.BlockSpec((1, 1, d), lambda i: (i // per_seq, 0, 0)),
            pl.BlockSpec((1, d), lambda i: (0, 0)),
        ],
        out_specs=pl.BlockSpec((tm, d), row),
        out_shape=jax.ShapeDtypeStruct((n, d), F32),
        compiler_params=_params("arbitrary"),
        name="output_projection",
    )(a, b, w, x2, gate, gain.reshape(1, d))


def _ffn_kernel(x_ref, g_ref, sc_ref, sh_ref, wu_ref, wg_ref, cw_ref, cb_ref, wd_ref,
                gate_ref, gpost_ref, o_ref, h_scr, act_scr, carry_scr, *, per_seq):
    i = pl.program_id(0)
    j = pl.program_id(1)
    tm, d = x_ref.shape
    tf = wd_ref.shape[0]

    @pl.when(j == 0)
    def _():
        h_scr[...] = _modulated_norm(x_ref, g_ref, sc_ref, sh_ref)
        o_ref[...] = jnp.zeros_like(o_ref)

    @pl.when((j == 0) & (i % per_seq == 0))
    def _():
        carry_scr[...] = jnp.zeros_like(carry_scr)

    h = h_scr[...]
    row = lax.broadcasted_iota(jnp.int32, (tm, FFN_CHUNK), 0)
    for c0 in range(0, tf, FFN_CHUNK):
        cols = slice(c0, c0 + FFN_CHUNK)
        u = jnp.dot(h, wu_ref[:, cols], preferred_element_type=F32)
        g = jnp.dot(h, wg_ref[:, cols], preferred_element_type=F32)
        tail = carry_scr[j, :, cols]
        carry_scr[j, :, cols] = g[tm - SUBLANES:, :]
        g1 = jnp.where(row == 0, tail[SUBLANES - 1:, :], pltpu.roll(g, 1, 0))
        g2 = jnp.where(row == 0, tail[SUBLANES - 2:SUBLANES - 1, :],
                       jnp.where(row == 1, tail[SUBLANES - 1:, :], pltpu.roll(g, 2, 0)))
        cw = cw_ref[:, cols]
        gc = cb_ref[:, cols] + cw[0:1] * g2 + cw[1:2] * g1 + cw[2:3] * g
        act = 0.5 * gc * (1.0 + jnp.tanh(math.sqrt(2.0 / math.pi) * (gc + 0.044715 * (gc * gc * gc))))
        act_scr[:, cols] = (act * u).astype(BF16)

    act = act_scr[...]
    for n0 in range(0, d, d // 2):
        out_cols = slice(n0, n0 + d // 2)
        o_ref[:, out_cols] += jnp.dot(act, wd_ref[:, out_cols], preferred_element_type=F32)

    @pl.when(j == pl.num_programs(1) - 1)
    def _():
        o_ref[...] = x_ref[...] + gate_ref[0] * _rms(o_ref[...], gpost_ref[...])


def _conv_ffn(x2, seq, gain, scale, shift, w_up, conv_w, conv_b, w_down, gate, gain_post, *, tm, tf):
    n, d = x2.shape
    d_ff = w_down.shape[0]
    nf = d_ff // tf
    per_seq = seq // tm
    row = lambda i, j: (i, 0)
    mod = lambda i, j: (i // per_seq, 0, 0)
    const = lambda i, j: (0, 0)
    return pl.pallas_call(
        functools.partial(_ffn_kernel, per_seq=per_seq),
        grid=(n // tm, nf),
        in_specs=[
            pl.BlockSpec((tm, d), row),
            pl.BlockSpec((1, d), const),
            pl.BlockSpec((1, 1, d), mod),
            pl.BlockSpec((1, 1, d), mod),
            pl.BlockSpec((d, tf), lambda i, j: (0, j)),
            pl.BlockSpec((d, tf), lambda i, j: (0, nf + j)),
            pl.BlockSpec((FFN_CONV, tf), lambda i, j: (0, j)),
            pl.BlockSpec((1, tf), lambda i, j: (0, j)),
            pl.BlockSpec((tf, d), lambda i, j: (j, 0)),
            pl.BlockSpec((1, 1, d), mod),
            pl.BlockSpec((1, d), const),
        ],
        out_specs=pl.BlockSpec((tm, d), row),
        out_shape=jax.ShapeDtypeStruct((n, d), F32),
        scratch_shapes=[
            pltpu.VMEM((tm, d), BF16),
            pltpu.VMEM((tm, tf), BF16),
            pltpu.VMEM((nf, SUBLANES, tf), F32),
        ],
        compiler_params=_params("arbitrary", "arbitrary"),
        name="conv_ffn",
    )(x2, gain.reshape(1, d), scale, shift, w_up, w_up, conv_w, conv_b.reshape(1, d_ff), w_down,
      gate, gain_post.reshape(1, d))


def _mlstm_kernel(qk_ref, v_ref, o_ref, gt_ref, cw_ref, cb_ref, gb_ref, mh_ref, out_ref,
                  tail_scr, c_scr, n_scr, m_scr):
    L = qk_ref.shape[0]
    heads = MLSTM_HEADS
    dqk = qk_ref.shape[1] // (2 * heads)
    dv = v_ref.shape[1] // heads

    @pl.when(pl.program_id(1) == 0)
    def _():
        tail_scr[...] = jnp.zeros_like(tail_scr)
        c_scr[...] = jnp.zeros_like(c_scr)
        n_scr[...] = jnp.zeros_like(n_scr)
        m_scr[...] = jnp.zeros_like(m_scr)

    raw = qk_ref[...]
    tail = tail_scr[...]
    tail_scr[...] = raw[L - SUBLANES:, :]
    row = lax.broadcasted_iota(jnp.int32, raw.shape, 0)
    cw = cw_ref[...]
    y = cb_ref[...] + cw[MLSTM_CONV - 1:MLSTM_CONV] * raw
    for lag in range(1, MLSTM_CONV):
        shifted = pltpu.roll(raw, lag, 0)
        for r in range(lag):
            shifted = jnp.where(row == r, tail[SUBLANES - lag + r:SUBLANES - lag + r + 1, :], shifted)
        y = y + cw[MLSTM_CONV - 1 - lag:MLSTM_CONV - lag] * shifted
    qk = y * _sigmoid(y)
    q_all = qk[:, :heads * dqk]
    k_all = qk[:, heads * dqk:] * (dqk ** -0.5)

    gates = gt_ref[...] + gb_ref[...]
    lf = _log_sigmoid(gates)
    r_i = lax.broadcasted_iota(jnp.int32, (L, L), 0)
    c_i = lax.broadcasted_iota(jnp.int32, (L, L), 1)
    causal = r_i >= c_i
    tri = causal.astype(F32)
    b_cols = jnp.dot(tri, lf, preferred_element_type=F32, precision=lax.Precision.HIGHEST)
    gates_t = gates.T
    lf_t = _log_sigmoid(gates_t)
    b_rows = lax.dot_general(lf_t, tri, NT, preferred_element_type=F32, precision=lax.Precision.HIGHEST)

    for h in range(heads):
        i_col = gates[:, h:h + 1]
        i_row = gates_t[h:h + 1, :]
        b_col = b_cols[:, heads + h:heads + h + 1]
        b_row = b_rows[heads + h:heads + h + 1, :]
        m_prev = m_scr[h][:, 0:1]
        n_prev = n_scr[h]
        c_prev = c_scr[h]
        qh = q_all[:, h * dqk:(h + 1) * dqk]
        kh = k_all[:, h * dqk:(h + 1) * dqk]
        vh = v_ref[:, h * dv:(h + 1) * dv].astype(BF16)
        qb = qh.astype(BF16)

        logd = jnp.where(causal, b_col - b_row + i_row, -jnp.inf)
        m_inter = m_prev + b_col
        m_row = jnp.maximum(m_inter, jnp.max(logd, axis=-1, keepdims=True))
        w_inter = jnp.exp(m_inter - m_row)
        s = lax.dot_general(qb, kh.astype(BF16), NT, preferred_element_type=F32) * jnp.exp(logd - m_row)
        num = (w_inter * jnp.dot(qb, c_prev.astype(BF16), preferred_element_type=F32)
               + jnp.dot(s.astype(BF16), vh, preferred_element_type=F32))
        den = w_inter * jnp.sum(qh * n_prev, axis=-1, keepdims=True) + jnp.sum(s, axis=-1, keepdims=True)
        hh = num / jnp.maximum(jnp.abs(den), jnp.exp(-m_row))

        b_last = b_col[L - 1:, :]
        logw = b_last - b_col + i_col
        m_new = jnp.maximum(m_prev + b_last, jnp.max(logw, axis=0, keepdims=True))
        w_old = jnp.exp(m_prev + b_last - m_new)
        wk = jnp.exp(logw - m_new) * kh
        c_scr[h] = w_old * c_prev + jnp.dot(wk.T.astype(BF16), vh, preferred_element_type=F32)
        n_scr[h] = w_old * n_prev + jnp.sum(wk, axis=0, keepdims=True)
        m_scr[h] = jnp.broadcast_to(m_new, (1, LANES))

        cols = slice(h * dv, (h + 1) * dv)
        out_ref[:, cols] = _rms(hh, mh_ref[:, cols]) * _sigmoid(o_ref[:, cols])


def _mlstm(proj, gates, conv_w, conv_b, gate_b, mh_norm, batch, seq):
    n = proj.shape[0]
    width = mh_norm.shape[0]
    heads = MLSTM_HEADS
    dqk = width // heads // 2
    L = MLSTM_CHUNK
    per_seq = seq // L
    blk = lambda col: (lambda b, c: (b * per_seq + c, col))
    const = lambda b, c: (0, 0)
    gate_b_pad = jnp.zeros((1, LANES), F32).at[0, :2 * heads].set(gate_b)
    return pl.pallas_call(
        _mlstm_kernel,
        grid=(batch, per_seq),
        in_specs=[
            pl.BlockSpec((L, width), blk(0)),
            pl.BlockSpec((L, width), blk(1)),
            pl.BlockSpec((L, width), blk(2)),
            pl.BlockSpec((L, LANES), blk(0)),
            pl.BlockSpec((MLSTM_CONV, width), const),
            pl.BlockSpec((1, width), const),
            pl.BlockSpec((1, LANES), const),
            pl.BlockSpec((1, width), const),
        ],
        out_specs=pl.BlockSpec((L, width), blk(0)),
        out_shape=jax.ShapeDtypeStruct((n, width), F32),
        scratch_shapes=[
            pltpu.VMEM((SUBLANES, width), F32),
            pltpu.VMEM((heads, dqk, width // heads), F32),
            pltpu.VMEM((heads, 1, dqk), F32),
            pltpu.VMEM((heads, 1, LANES), F32),
        ],
        compiler_params=_params("arbitrary", "arbitrary"),
        name="mlstm",
    )(proj, proj, proj, gates, conv_w, conv_b.reshape(1, width), gate_b_pad, mh_norm.reshape(1, width))


MOBA_SPLIT = 3
MOBA_EXTRA = HEAD_DIM


def _split_bf16(x):
    terms = []
    for _ in range(MOBA_SPLIT - 1):
        t = x.astype(BF16).astype(F32)
        terms.append(t)
        x = x - t
    terms.append(x)
    return terms


def _moba_prepare_kernel(k_ref, v_ref, km_ref, ka_ref, vt_ref):
    nblk = km_ref.shape[1]
    bs = MOBA_BLOCK
    seq, kvw = k_ref.shape
    k = k_ref[...]
    km_ref[0] = jnp.mean(k.reshape(nblk, bs, kvw), axis=1)
    col = lax.broadcasted_iota(jnp.int32, (seq, MOBA_EXTRA), 1)
    row = lax.broadcasted_iota(jnp.int32, (seq, MOBA_EXTRA), 0)
    in_onehot = col < MOBA_SPLIT * nblk
    in_offset = (col >= MOBA_SPLIT * nblk) & (col < MOBA_SPLIT * (nblk + 1))
    onehot = jnp.where(in_onehot & (col % nblk == row // bs), 1.0, 0.0)
    extra = jnp.where(in_offset, (row % bs).astype(F32), onehot)
    for kv in range(kvw // HEAD_DIM):
        ka_ref[0, kv] = jnp.concatenate([k[:, kv * HEAD_DIM:(kv + 1) * HEAD_DIM], extra], axis=1).astype(BF16)
    for j in range(nblk):
        vt_ref[0, j] = v_ref[j * bs:(j + 1) * bs, :].T.astype(BF16)


def _moba_prepare(proj, batch, seq, *, k_col, v_col, kvw):
    nblk = seq // MOBA_BLOCK
    kv_heads = kvw // HEAD_DIM
    return pl.pallas_call(
        _moba_prepare_kernel,
        grid=(batch,),
        in_specs=[pl.BlockSpec((seq, kvw), lambda b: (b, k_col)),
                  pl.BlockSpec((seq, kvw), lambda b: (b, v_col))],
        out_specs=(pl.BlockSpec((1, nblk, kvw), lambda b: (b, 0, 0)),
                   pl.BlockSpec((1, kv_heads, seq, HEAD_DIM + MOBA_EXTRA), lambda b: (b, 0, 0, 0)),
                   pl.BlockSpec((1, nblk, kvw, MOBA_BLOCK), lambda b: (b, 0, 0, 0))),
        out_shape=(jax.ShapeDtypeStruct((batch, nblk, kvw), F32),
                   jax.ShapeDtypeStruct((batch, kv_heads, seq, HEAD_DIM + MOBA_EXTRA), BF16),
                   jax.ShapeDtypeStruct((batch, nblk, kvw, MOBA_BLOCK), BF16)),
        compiler_params=_params("arbitrary"),
        name="moba_prepare",
    )(proj, proj)


def _moba_kernel(q_ref, ka_ref, vt_ref, km_ref, o_ref, qa_scr, acc_scr, m_scr, l_scr, ot_scr, *, slopes):
    n = pl.program_id(1)
    bs = MOBA_BLOCK
    nblk = km_ref.shape[1]
    kv_heads = ka_ref.shape[1]
    cols = GQA_GROUP * bs

    lane = lax.broadcasted_iota(jnp.int32, (1, cols), 1)
    head_of_lane = lane // bs
    blk_t = lax.broadcasted_iota(jnp.int32, (nblk, cols), 0)
    dist0 = ((n - blk_t) * bs + (lane & (bs - 1))).astype(F32)
    sub = lax.broadcasted_iota(jnp.int32, (SUBLANES, cols), 0)
    pad = jnp.zeros((MOBA_EXTRA - MOBA_SPLIT * nblk - SUBLANES, cols), F32)

    q_t = q_ref[...].T

    for kv in range(kv_heads):
        slope = jnp.zeros((1, cols), F32)
        for g in range(GQA_GROUP):
            slope = jnp.where(head_of_lane == g, slopes[kv * GQA_GROUP + g], slope)
        heads = [q_t[(kv * GQA_GROUP + g) * HEAD_DIM:(kv * GQA_GROUP + g + 1) * HEAD_DIM, :]
                 for g in range(GQA_GROUP)]
        qf_t = jnp.concatenate(heads, axis=1) * HEAD_DIM ** -0.5

        gate_t = jnp.dot(km_ref[0][:, kv * HEAD_DIM:(kv + 1) * HEAD_DIM], qf_t,
                         preferred_element_type=F32, precision=lax.Precision.HIGHEST)
        rank = jnp.zeros((nblk, cols), jnp.int32)
        for c in range(nblk):
            gc = gate_t[c:c + 1, :]
            beats = (gc > gate_t) | ((gc == gate_t) & (c < blk_t))
            rank = rank + jnp.where(beats, jnp.where(c < n, 1, 0), 0)
        chosen = ((blk_t < n) & (rank < MOBA_TOPK)) | (blk_t == n)
        bias_t = jnp.where(chosen, 0.0, NEG_INF) - slope * dist0
        slope_rows = jnp.zeros((SUBLANES, cols), F32)
        for t, term in enumerate(_split_bf16(slope)):
            slope_rows = jnp.where(sub == t, term, slope_rows)
        qa_scr[kv] = jnp.concatenate([qf_t] + _split_bf16(bias_t) + [slope_rows, pad], axis=0).astype(BF16)

    m_scr[...] = jnp.full_like(m_scr, NEG_INF)
    l_scr[...] = jnp.zeros_like(l_scr)
    acc_scr[...] = jnp.zeros_like(acc_scr)

    def attend(j0, nb, mask):
        start = pl.multiple_of(j0 * bs, bs)
        for kv in range(kv_heads):
            s = jnp.dot(ka_ref[0, kv, pl.ds(start, nb * bs), :], qa_scr[kv], preferred_element_type=F32)
            if mask is not None:
                s = jnp.where(mask, s, NEG_INF)
            m_old = m_scr[kv]
            m_new = jnp.maximum(m_old, jnp.max(s, axis=0, keepdims=True))
            alpha = jnp.exp(m_old - m_new)
            p = jnp.exp(s - m_new)
            v_t = jnp.concatenate([vt_ref[0, j0 + t, kv * HEAD_DIM:(kv + 1) * HEAD_DIM, :] for t in range(nb)],
                                  axis=1)
            acc_scr[kv] = alpha * acc_scr[kv] + jnp.dot(v_t, p.astype(BF16), preferred_element_type=F32)
            l_scr[kv] = alpha * l_scr[kv] + jnp.sum(p, axis=0, keepdims=True)
            m_scr[kv] = m_new

    tok = lax.broadcasted_iota(jnp.int32, (1, cols), 1) & (bs - 1)
    odd = (n + 1) % 2

    @pl.when(odd == 1)
    def _():
        visible = jnp.where(n > 0, bs, 0)
        key1 = lax.broadcasted_iota(jnp.int32, (bs, cols), 0)
        attend(0, 1, tok + visible >= key1)

    def body(i, carry):
        attend(odd + 2 * i, 2, None)
        return carry

    lax.fori_loop(0, (n + 1) // 2 - 1, body, 0)

    @pl.when(n >= 1)
    def _():
        key2 = lax.broadcasted_iota(jnp.int32, (2 * bs, cols), 0)
        attend(n - 1, 2, tok + bs >= key2)

    for kv in range(kv_heads):
        o_t = acc_scr[kv] / l_scr[kv]
        for g in range(GQA_GROUP):
            h = kv * GQA_GROUP + g
            ot_scr[h * HEAD_DIM:(h + 1) * HEAD_DIM, :] = o_t[:, g * bs:(g + 1) * bs]
    o_ref[...] = ot_scr[...].T


def _moba_attention(proj, batch, seq, *, width, q_col, k_col, v_col):
    n = proj.shape[0]
    kvw = width // GQA_GROUP
    kv_heads = kvw // HEAD_DIM
    bs = MOBA_BLOCK
    nblk = seq // bs
    assert MOBA_SPLIT * nblk + SUBLANES <= MOBA_EXTRA
    k_means, k_aug, v_t = _moba_prepare(proj, batch, seq, k_col=k_col, v_col=v_col, kvw=kvw)
    cols = GQA_GROUP * bs
    return pl.pallas_call(
        functools.partial(_moba_kernel, slopes=_alibi_slopes(width // HEAD_DIM)),
        grid=(batch, nblk),
        in_specs=[
            pl.BlockSpec((bs, width), lambda b, i: (b * nblk + i, q_col)),
            pl.BlockSpec((1, kv_heads, seq, HEAD_DIM + MOBA_EXTRA), lambda b, i: (b, 0, 0, 0)),
            pl.BlockSpec((1, nblk, kvw, bs), lambda b, i: (b, 0, 0, 0)),
            pl.BlockSpec((1, nblk, kvw), lambda b, i: (b, 0, 0)),
        ],
        out_specs=pl.BlockSpec((bs, width), lambda b, i: (b * nblk + i, 0)),
        out_shape=jax.ShapeDtypeStruct((n, width), F32),
        scratch_shapes=[
            pltpu.VMEM((kv_heads, HEAD_DIM + MOBA_EXTRA, cols), BF16),
            pltpu.VMEM((kv_heads, HEAD_DIM, cols), F32),
            pltpu.VMEM((kv_heads, 1, cols), F32),
            pltpu.VMEM((kv_heads, 1, cols), F32),
            pltpu.VMEM((width, bs), F32),
        ],
        compiler_params=_params("arbitrary", "arbitrary"),
        name="moba_attention",
    )(proj, k_aug, v_t, k_means)


def kernel(x, c, ada_w, ada_b, norm_mix_pre, norm_mix_post, norm_ffn_pre, norm_ffn_post, ffn_w_up, ffn_conv_w, ffn_conv_b, ffn_w_down, ev_w_in, ev_w_out, ev_sinks, ev_pool_w, ev_pool_b, ev_pool_scale, od_w_in, od_w_out, od_conv_w, od_conv_b, od_gate_b, od_mh_norm):
    batch, seq, d = x.shape
    depth = ada_w.shape[0]
    assert seq % 512 == 0
    x2 = x.reshape(batch * seq, d)

    mod = _ada_modulation(c, ada_w, ada_b)
    mod = mod.reshape(depth, batch, 6, 1, d)

    for layer in range(depth):
        sh_m, sc_m, gt_m, sh_f, sc_f, gt_f = [mod[layer, :, r] for r in range(6)]
        j = layer // 2
        if layer % 2 == 0:
            width = ev_pool_scale.shape[1]
            kvw = width // GQA_GROUP
            w_in = ev_w_in[j]
            w_in = jnp.concatenate([w_in[:, :width], w_in[:, width + 2 * kvw:],
                                    w_in[:, width:width + 2 * kvw]], axis=1).astype(BF16)
            proj = _input_projection(x2, seq, norm_mix_pre[layer], sc_m, sh_m, w_in, tm=1024, tn=1280)
            a = _sliding_window_attention(proj, ev_sinks[j], batch, seq, width=width,
                                          k_col=2 * width // kvw, v_col=2 * width // kvw + 1, tq=512)
            b = _pool_mixer(proj, ev_pool_w[j].astype(BF16), ev_pool_b[j], ev_pool_scale[j], batch, seq,
                            col=1, tt=512)
            w_out = ev_w_out[j].astype(BF16)
        else:
            width = od_mh_norm.shape[1]
            kvw = width // GQA_GROUP
            heads = MLSTM_HEADS
            w_in = od_w_in[j]
            g0 = 3 * width
            w_main = jnp.concatenate([w_in[:, :g0], w_in[:, g0 + 2 * heads:]], axis=1).astype(BF16)
            w_gates = jnp.zeros((d, LANES), BF16).at[:, :2 * heads].set(
                w_in[:, g0:g0 + 2 * heads].astype(BF16))
            proj, gates = _input_projection(x2, seq, norm_mix_pre[layer], sc_m, sh_m, w_main, w_gates,
                                            tm=1024, tn=1536)
            a = _mlstm(proj, gates, od_conv_w[j], od_conv_b[j], od_gate_b[j], od_mh_norm[j], batch, seq)
            b = _moba_attention(proj, batch, seq, width=width, q_col=3,
                                k_col=4 * width // kvw, v_col=4 * width // kvw + 1)
            w_out = od_w_out[j].astype(BF16)
        x2 = _output_projection(a, b, w_out, x2, seq, gt_m, norm_mix_post[layer], tm=512)
        x2 = _conv_ffn(x2, seq, norm_ffn_pre[layer], sc_f, sh_f, ffn_w_up[layer].astype(BF16),
                       ffn_conv_w[layer], ffn_conv_b[layer], ffn_w_down[layer].astype(BF16),
                       gt_f, norm_ffn_post[layer], tm=512, tf=512)
    return x2.reshape(batch, seq, d)
```

```python
import functools
import math

import jax
import jax.numpy as jnp
from jax import lax
from jax.experimental import pallas as pl
from jax.experimental.pallas import tpu as pltpu

F32 = jnp.float32
BF16 = jnp.bfloat16

HEAD_DIM = 64
GQA_GROUP = 4
SWA_WINDOW = 128
POOL_WINDOWS = (2, 4, 8, 16)
POOL_HALO = 16
MLSTM_HEADS = 4
MLSTM_CONV = 4
MLSTM_CHUNK = 256
MOBA_BLOCK = 256
MOBA_TOPK = 3
FFN_CONV = 3
FFN_CHUNK = 256
NORM_EPS = 1e-6
NEG_INF = -1e30
LANES = 128
SUBLANES = 8
VMEM_LIMIT = 60 * 1024 * 1024

NT = (((1,), (1,)), ((), ()))


def _alibi_slopes(n_heads):
    return [2.0 ** (-8.0 * (h + 1) / n_heads) for h in range(n_heads)]


def _params(*semantics):
    return pltpu.CompilerParams(dimension_semantics=semantics, vmem_limit_bytes=VMEM_LIMIT)


def _rms(x, gain):
    ms = jnp.mean(x * x, axis=-1, keepdims=True)
    return x * lax.rsqrt(ms + NORM_EPS) * gain


def _sigmoid(x):
    return 1.0 / (1.0 + jnp.exp(-x))


def _log_sigmoid(x):
    return jnp.minimum(x, 0.0) - jnp.log(1.0 + jnp.exp(-jnp.abs(x)))


def _ada_kernel(c_ref, w_ref, b_ref, o_ref):
    c = c_ref[...]
    a = (c * _sigmoid(c)).astype(BF16)
    o_ref[0] = jnp.dot(a, w_ref[0].astype(BF16), preferred_element_type=F32) + b_ref[0]


def _ada_modulation(c, ada_w, ada_b):
    depth, d, n_out = ada_w.shape
    rows = 2 * SUBLANES
    c_pad = jnp.zeros((rows, d), F32).at[:c.shape[0]].set(c)
    tn = 1024
    out = pl.pallas_call(
        _ada_kernel,
        grid=(depth, n_out // tn),
        in_specs=[
            pl.BlockSpec((rows, d), lambda l, j: (0, 0)),
            pl.BlockSpec((1, d, tn), lambda l, j: (l, 0, j)),
            pl.BlockSpec((1, 1, tn), lambda l, j: (l, 0, j)),
        ],
        out_specs=pl.BlockSpec((1, rows, tn), lambda l, j: (l, 0, j)),
        out_shape=jax.ShapeDtypeStruct((depth, rows, n_out), F32),
        compiler_params=_params("arbitrary", "arbitrary"),
        name="ada_modulation",
    )(c_pad, ada_w, ada_b.reshape(depth, 1, n_out))
    return out[:, :c.shape[0]]


def _modulated_norm(x_ref, g_ref, sc_ref, sh_ref):
    return (_rms(x_ref[...], g_ref[...]) * (1.0 + sc_ref[0]) + sh_ref[0]).astype(BF16)


def _inproj_kernel(x_ref, g_ref, sc_ref, sh_ref, w_ref, o_ref, h_scr):
    @pl.when(pl.program_id(1) == 0)
    def _():
        h_scr[...] = _modulated_norm(x_ref, g_ref, sc_ref, sh_ref)

    o_ref[...] = jnp.dot(h_scr[...], w_ref[...], preferred_element_type=F32)


def _inproj_gates_kernel(x_ref, g_ref, sc_ref, sh_ref, w_ref, wg_ref, o_ref, og_ref, h_scr):
    @pl.when(pl.program_id(1) == 0)
    def _():
        h = _modulated_norm(x_ref, g_ref, sc_ref, sh_ref)
        h_scr[...] = h
        og_ref[...] = jnp.dot(h, wg_ref[...], preferred_element_type=F32)

    o_ref[...] = jnp.dot(h_scr[...], w_ref[...], preferred_element_type=F32)


def _input_projection(x2, seq, gain, scale, shift, w, w_gates=None, *, tm, tn):
    n, d = x2.shape
    n_out = w.shape[1]
    per_seq = seq // tm
    row = lambda i, j: (i, 0)
    mod = lambda i, j: (i // per_seq, 0, 0)
    in_specs = [
        pl.BlockSpec((tm, d), row),
        pl.BlockSpec((1, d), lambda i, j: (0, 0)),
        pl.BlockSpec((1, 1, d), mod),
        pl.BlockSpec((1, 1, d), mod),
        pl.BlockSpec((d, tn), lambda i, j: (0, j)),
    ]
    args = [x2, gain.reshape(1, d), scale, shift, w]
    out_specs = pl.BlockSpec((tm, tn), lambda i, j: (i, j))
    out_shape = jax.ShapeDtypeStruct((n, n_out), F32)
    body = _inproj_kernel
    if w_gates is not None:
        in_specs.append(pl.BlockSpec((d, LANES), lambda i, j: (0, 0)))
        args.append(w_gates)
        out_specs = (out_specs, pl.BlockSpec((tm, LANES), row))
        out_shape = (out_shape, jax.ShapeDtypeStruct((n, LANES), F32))
        body = _inproj_gates_kernel
    return pl.pallas_call(
        body,
        grid=(n // tm, n_out // tn),
        in_specs=in_specs,
        out_specs=out_specs,
        out_shape=out_shape,
        scratch_shapes=[pltpu.VMEM((tm, d), BF16)],
        compiler_params=_params("arbitrary", "arbitrary"),
        name="input_projection",
    )(*args)


def _stack_heads(rows, scale):
    parts = [rows[:, g * HEAD_DIM:(g + 1) * HEAD_DIM] for g in range(GQA_GROUP)]
    return jnp.concatenate(parts, axis=0) * scale


def _swa_kernel(sinks_ref, q_ref, kc_ref, vc_ref, kp_ref, vp_ref, o_ref, *, slopes):
    first_tile = pl.program_id(1) == 0
    w = SWA_WINDOW
    g_rows = GQA_GROUP * w
    kv_heads = kc_ref.shape[1] // HEAD_DIM
    row = lax.broadcasted_iota(jnp.int32, (g_rows, 2 * w), 0)
    ki = lax.broadcasted_iota(jnp.int32, (g_rows, 2 * w), 1)
    dist = (row & (w - 1)) + w - ki
    band = (dist >= 0) & (dist < w)
    first_key = jnp.where(first_tile, w, 0)
    distf = dist.astype(F32)
    head_of_row = row // w
    head_col = lax.broadcasted_iota(jnp.int32, (g_rows, 1), 0) // w
    for kv in range(kv_heads):
        slope = jnp.zeros((g_rows, 2 * w), F32)
        sink = jnp.zeros((g_rows, 1), F32)
        for g in range(GQA_GROUP):
            h = kv * GQA_GROUP + g
            slope = jnp.where(head_of_row == g, slopes[h], slope)
            sink = jnp.where(head_col == g, sinks_ref[h], sink)
        bias = -slope * distf
        lanes = slice(kv * HEAD_DIM, (kv + 1) * HEAD_DIM)
        for sb in range(q_ref.shape[0] // w):
            rows = slice(sb * w, (sb + 1) * w)
            q = _stack_heads(q_ref[rows, kv * GQA_GROUP * HEAD_DIM:(kv + 1) * GQA_GROUP * HEAD_DIM],
                             HEAD_DIM ** -0.5).astype(BF16)
            if sb == 0:
                k2 = jnp.concatenate([kp_ref[:, lanes], kc_ref[0:w, lanes]], axis=0)
                v2 = jnp.concatenate([vp_ref[:, lanes], vc_ref[0:w, lanes]], axis=0)
                mask = band & (ki >= first_key)
            else:
                k2 = kc_ref[(sb - 1) * w:(sb + 1) * w, lanes]
                v2 = vc_ref[(sb - 1) * w:(sb + 1) * w, lanes]
                mask = band
            s = lax.dot_general(q, k2.astype(BF16), NT, preferred_element_type=F32)
            logits = jnp.where(mask, s + bias, NEG_INF)
            m = jnp.maximum(jnp.max(logits, axis=-1, keepdims=True), sink)
            e = jnp.exp(logits - m)
            denom = jnp.sum(e, axis=-1, keepdims=True) + jnp.exp(sink - m)
            o = jnp.dot(e.astype(BF16), v2.astype(BF16), preferred_element_type=F32) / denom
            for g in range(GQA_GROUP):
                h = kv * GQA_GROUP + g
                o_ref[rows, h * HEAD_DIM:(h + 1) * HEAD_DIM] = o[g * w:(g + 1) * w]


def _sliding_window_attention(proj, sinks, batch, seq, *, width, k_col, v_col, tq):
    n = proj.shape[0]
    kvw = width // GQA_GROUP
    per_seq = seq // tq
    sub = tq // SWA_WINDOW
    cur = lambda col: (lambda b, i: (b * per_seq + i, col))
    prev = lambda col: (lambda b, i: (jnp.maximum((b * per_seq + i) * sub - 1, 0), col))
    return pl.pallas_call(
        functools.partial(_swa_kernel, slopes=_alibi_slopes(width // HEAD_DIM)),
        grid=(batch, per_seq),
        in_specs=[
            pl.BlockSpec(memory_space=pltpu.SMEM),
            pl.BlockSpec((tq, width), cur(0)),
            pl.BlockSpec((tq, kvw), cur(k_col)),
            pl.BlockSpec((tq, kvw), cur(v_col)),
            pl.BlockSpec((SWA_WINDOW, kvw), prev(k_col)),
            pl.BlockSpec((SWA_WINDOW, kvw), prev(v_col)),
        ],
        out_specs=pl.BlockSpec((tq, width), cur(0)),
        out_shape=jax.ShapeDtypeStruct((n, width), F32),
        compiler_params=_params("arbitrary", "arbitrary"),
        name="sliding_window_attention",
    )(sinks, proj, proj, proj, proj, proj)


def _pool_kernel(p_ref, halo_ref, w_ref, b_ref, sc_ref, o_ref):
    tt = p_ref.shape[0]
    gw = w_ref.shape[1]
    i = pl.program_id(1)
    pos1 = (i * tt + 1 + lax.broadcasted_iota(jnp.int32, (tt, 1), 0)).astype(F32)
    for g, win in enumerate(POOL_WINDOWS):
        lanes = slice(g * gw, (g + 1) * gw)
        cur = p_ref[:, lanes]
        halo = jnp.where(i > 0, halo_ref[:, lanes], 0.0)
        a = jnp.concatenate([halo, cur], axis=0)
        step = 1
        while step < win:
            a = a[step:] + a[:-step]
            step *= 2
        off = POOL_HALO - (win - 1)
        mean = a[off:off + tt] / jnp.minimum(pos1, float(win))
        d = (mean - cur).astype(BF16)
        y = jnp.dot(d, w_ref[g], preferred_element_type=F32) + b_ref[g]
        o_ref[:, lanes] = y * sc_ref[:, lanes]


def _pool_mixer(proj, pool_w, pool_b, pool_scale, batch, seq, *, col, tt):
    n = proj.shape[0]
    groups, gw, _ = pool_w.shape
    width = groups * gw
    per_seq = seq // tt
    halo_blocks = tt // POOL_HALO
    return pl.pallas_call(
        _pool_kernel,
        grid=(batch, per_seq),
        in_specs=[
            pl.BlockSpec((tt, width), lambda b, i: (b * per_seq + i, col)),
            pl.BlockSpec((POOL_HALO, width),
                         lambda b, i: (jnp.maximum((b * per_seq + i) * halo_blocks - 1, 0), col)),
            pl.BlockSpec((groups, gw, gw), lambda b, i: (0, 0, 0)),
            pl.BlockSpec((groups, 1, gw), lambda b, i: (0, 0, 0)),
            pl.BlockSpec((1, width), lambda b, i: (0, 0)),
        ],
        out_specs=pl.BlockSpec((tt, width), lambda b, i: (b * per_seq + i, 0)),
        out_shape=jax.ShapeDtypeStruct((n, width), F32),
        compiler_params=_params("arbitrary", "arbitrary"),
        name="pool_mixer",
    )(proj, proj, pool_w, pool_b.reshape(groups, 1, gw), pool_scale.reshape(1, width))


def _outproj_kernel(a_ref, b_ref, w_ref, x_ref, gate_ref, g_ref, o_ref):
    half = a_ref.shape[1]
    y = jnp.dot(a_ref[...].astype(BF16), w_ref[0:half, :], preferred_element_type=F32)
    y = y + jnp.dot(b_ref[...].astype(BF16), w_ref[half:, :], preferred_element_type=F32)
    o_ref[...] = x_ref[...] + gate_ref[0] * _rms(y, g_ref[...])


def _output_projection(a, b, w, x2, seq, gate, gain, *, tm):
    n, d = x2.shape
    half = a.shape[1]
    per_seq = seq // tm
    row = lambda i: (i, 0)
    return pl.pallas_call(
        _outproj_kernel,
        grid=(n // tm,),
        in_specs=[
            pl.BlockSpec((tm, half), row),
            pl.BlockSpec((tm, half), row),
            pl.BlockSpec((2 * half, d), lambda i: (0, 0)),
            pl.BlockSpec((tm, d), row),
            pl.BlockSpec((1, 1, d), lambda i: (i // per_seq, 0, 0)),
            pl.BlockSpec((1, d), lambda i: (0, 0)),
        ],
        out_specs=pl.BlockSpec((tm, d), row),
        out_shape=jax.ShapeDtypeStruct((n, d), F32),
        compiler_params=_params("arbitrary"),
        name="output_projection",
    )(a, b, w, x2, gate, gain.reshape(1, d))


def _ffn_kernel(x_ref, g_ref, sc_ref, sh_ref, wu_ref, wg_ref, cw_ref, cb_ref, wd_ref,
                gate_ref, gpost_ref, o_ref, h_scr, act_scr, carry_scr, *, per_seq):
    i = pl.program_id(0)
    j = pl.program_id(1)
    tm, d = x_ref.shape
    tf = wd_ref.shape[0]

    @pl.when(j == 0)
    def _():
        h_scr[...] = _modulated_norm(x_ref, g_ref, sc_ref, sh_ref)
        o_ref[...] = jnp.zeros_like(o_ref)

    @pl.when((j == 0) & (i % per_seq == 0))
    def _():
        carry_scr[...] = jnp.zeros_like(carry_scr)

    h = h_scr[...]
    row = lax.broadcasted_iota(jnp.int32, (tm, FFN_CHUNK), 0)
    for c0 in range(0, tf, FFN_CHUNK):
        cols = slice(c0, c0 + FFN_CHUNK)
        u = jnp.dot(h, wu_ref[:, cols], preferred_element_type=F32)
        g = jnp.dot(h, wg_ref[:, cols], preferred_element_type=F32)
        tail = carry_scr[j, :, cols]
        carry_scr[j, :, cols] = g[tm - SUBLANES:, :]
        g1 = jnp.where(row == 0, tail[SUBLANES - 1:, :], pltpu.roll(g, 1, 0))
        g2 = jnp.where(row == 0, tail[SUBLANES - 2:SUBLANES - 1, :],
                       jnp.where(row == 1, tail[SUBLANES - 1:, :], pltpu.roll(g, 2, 0)))
        cw = cw_ref[:, cols]
        gc = cb_ref[:, cols] + cw[0:1] * g2 + cw[1:2] * g1 + cw[2:3] * g
        act = 0.5 * gc * (1.0 + jnp.tanh(math.sqrt(2.0 / math.pi) * (gc + 0.044715 * (gc * gc * gc))))
        act_scr[:, cols] = (act * u).astype(BF16)

    act = act_scr[...]
    for n0 in range(0, d, d // 2):
        out_cols = slice(n0, n0 + d // 2)
        o_ref[:, out_cols] += jnp.dot(act, wd_ref[:, out_cols], preferred_element_type=F32)

    @pl.when(j == pl.num_programs(1) - 1)
    def _():
        o_ref[...] = x_ref[...] + gate_ref[0] * _rms(o_ref[...], gpost_ref[...])


def _conv_ffn(x2, seq, gain, scale, shift, w_up, conv_w, conv_b, w_down, gate, gain_post, *, tm, tf):
    n, d = x2.shape
    d_ff = w_down.shape[0]
    nf = d_ff // tf
    per_seq = seq // tm
    row = lambda i, j: (i, 0)
    mod = lambda i, j: (i // per_seq, 0, 0)
    const = lambda i, j: (0, 0)
    return pl.pallas_call(
        functools.partial(_ffn_kernel, per_seq=per_seq),
        grid=(n // tm, nf),
        in_specs=[
            pl.BlockSpec((tm, d), row, pipeline_mode=pl.Buffered(1)),
            pl.BlockSpec((1, d), const),
            pl.BlockSpec((1, 1, d), mod),
            pl.BlockSpec((1, 1, d), mod),
            pl.BlockSpec((d, tf), lambda i, j: (0, j)),
            pl.BlockSpec((d, tf), lambda i, j: (0, nf + j)),
            pl.BlockSpec((FFN_CONV, tf), lambda i, j: (0, j)),
            pl.BlockSpec((1, tf), lambda i, j: (0, j)),
            pl.BlockSpec((tf, d), lambda i, j: (j, 0)),
            pl.BlockSpec((1, 1, d), mod),
            pl.BlockSpec((1, d), const),
        ],
        out_specs=pl.BlockSpec((tm, d), row),
        out_shape=jax.ShapeDtypeStruct((n, d), F32),
        scratch_shapes=[
            pltpu.VMEM((tm, d), BF16),
            pltpu.VMEM((tm, tf), BF16),
            pltpu.VMEM((nf, SUBLANES, tf), F32),
        ],
        compiler_params=_params("arbitrary", "arbitrary"),
        name="conv_ffn",
    )(x2, gain.reshape(1, d), scale, shift, w_up, w_up, conv_w, conv_b.reshape(1, d_ff), w_down,
      gate, gain_post.reshape(1, d))


def _mlstm_kernel(qk_ref, v_ref, o_ref, gt_ref, cw_ref, cb_ref, gb_ref, mh_ref, out_ref,
                  tail_scr, c_scr, n_scr, m_scr):
    L = qk_ref.shape[0]
    heads = MLSTM_HEADS
    dqk = qk_ref.shape[1] // (2 * heads)
    dv = v_ref.shape[1] // heads

    @pl.when(pl.program_id(1) == 0)
    def _():
        tail_scr[...] = jnp.zeros_like(tail_scr)
        c_scr[...] = jnp.zeros_like(c_scr)
        n_scr[...] = jnp.zeros_like(n_scr)
        m_scr[...] = jnp.zeros_like(m_scr)

    raw = qk_ref[...]
    tail = tail_scr[...]
    tail_scr[...] = raw[L - SUBLANES:, :]
    row = lax.broadcasted_iota(jnp.int32, raw.shape, 0)
    cw = cw_ref[...]
    y = cb_ref[...] + cw[MLSTM_CONV - 1:MLSTM_CONV] * raw
    for lag in range(1, MLSTM_CONV):
        shifted = pltpu.roll(raw, lag, 0)
        for r in range(lag):
            shifted = jnp.where(row == r, tail[SUBLANES - lag + r:SUBLANES - lag + r + 1, :], shifted)
        y = y + cw[MLSTM_CONV - 1 - lag:MLSTM_CONV - lag] * shifted
    qk = y * _sigmoid(y)
    q_all = qk[:, :heads * dqk]
    k_all = qk[:, heads * dqk:] * (dqk ** -0.5)

    gates = gt_ref[...] + gb_ref[...]
    lf = _log_sigmoid(gates)
    r_i = lax.broadcasted_iota(jnp.int32, (L, L), 0)
    c_i = lax.broadcasted_iota(jnp.int32, (L, L), 1)
    causal = r_i >= c_i
    tri = causal.astype(F32)
    b_cols = jnp.dot(tri, lf, preferred_element_type=F32, precision=lax.Precision.HIGHEST)
    gates_t = gates.T
    lf_t = _log_sigmoid(gates_t)
    b_rows = lax.dot_general(lf_t, tri, NT, preferred_element_type=F32, precision=lax.Precision.HIGHEST)

    for h in range(heads):
        i_col = gates[:, h:h + 1]
        i_row = gates_t[h:h + 1, :]
        b_col = b_cols[:, heads + h:heads + h + 1]
        b_row = b_rows[heads + h:heads + h + 1, :]
        m_prev = m_scr[h][:, 0:1]
        n_prev = n_scr[h]
        c_prev = c_scr[h]
        qh = q_all[:, h * dqk:(h + 1) * dqk]
        kh = k_all[:, h * dqk:(h + 1) * dqk]
        vh = v_ref[:, h * dv:(h + 1) * dv].astype(BF16)
        qb = qh.astype(BF16)

        logd = jnp.where(causal, b_col - b_row + i_row, -jnp.inf)
        m_inter = m_prev + b_col
        m_row = jnp.maximum(m_inter, jnp.max(logd, axis=-1, keepdims=True))
        w_inter = jnp.exp(m_inter - m_row)
        s = lax.dot_general(qb, kh.astype(BF16), NT, preferred_element_type=F32) * jnp.exp(logd - m_row)
        num = (w_inter * jnp.dot(qb, c_prev.astype(BF16), preferred_element_type=F32)
               + jnp.dot(s.astype(BF16), vh, preferred_element_type=F32))
        den = w_inter * jnp.sum(qh * n_prev, axis=-1, keepdims=True) + jnp.sum(s, axis=-1, keepdims=True)
        hh = num / jnp.maximum(jnp.abs(den), jnp.exp(-m_row))

        b_last = b_col[L - 1:, :]
        logw = b_last - b_col + i_col
        m_new = jnp.maximum(m_prev + b_last, jnp.max(logw, axis=0, keepdims=True))
        w_old = jnp.exp(m_prev + b_last - m_new)
        wk = jnp.exp(logw - m_new) * kh
        c_scr[h] = w_old * c_prev + jnp.dot(wk.T.astype(BF16), vh, preferred_element_type=F32)
        n_scr[h] = w_old * n_prev + jnp.sum(wk, axis=0, keepdims=True)
        m_scr[h] = jnp.broadcast_to(m_new, (1, LANES))

        cols = slice(h * dv, (h + 1) * dv)
        out_ref[:, cols] = _rms(hh, mh_ref[:, cols]) * _sigmoid(o_ref[:, cols])


def _mlstm(proj, gates, conv_w, conv_b, gate_b, mh_norm, batch, seq):
    n = proj.shape[0]
    width = mh_norm.shape[0]
    heads = MLSTM_HEADS
    dqk = width // heads // 2
    L = MLSTM_CHUNK
    per_seq = seq // L
    blk = lambda col: (lambda b, c: (b * per_seq + c, col))
    const = lambda b, c: (0, 0)
    gate_b_pad = jnp.zeros((1, LANES), F32).at[0, :2 * heads].set(gate_b)
    return pl.pallas_call(
        _mlstm_kernel,
        grid=(batch, per_seq),
        in_specs=[
            pl.BlockSpec((L, width), blk(0)),
            pl.BlockSpec((L, width), blk(1)),
            pl.BlockSpec((L, width), blk(2)),
            pl.BlockSpec((L, LANES), blk(0)),
            pl.BlockSpec((MLSTM_CONV, width), const),
            pl.BlockSpec((1, width), const),
            pl.BlockSpec((1, LANES), const),
            pl.BlockSpec((1, width), const),
        ],
        out_specs=pl.BlockSpec((L, width), blk(0)),
        out_shape=jax.ShapeDtypeStruct((n, width), F32),
        scratch_shapes=[
            pltpu.VMEM((SUBLANES, width), F32),
            pltpu.VMEM((heads, dqk, width // heads), F32),
            pltpu.VMEM((heads, 1, dqk), F32),
            pltpu.VMEM((heads, 1, LANES), F32),
        ],
        compiler_params=_params("arbitrary", "arbitrary"),
        name="mlstm",
    )(proj, proj, proj, gates, conv_w, conv_b.reshape(1, width), gate_b_pad, mh_norm.reshape(1, width))


MOBA_SPLIT = 3
MOBA_EXTRA = HEAD_DIM


def _split_bf16(x):
    terms = []
    for _ in range(MOBA_SPLIT - 1):
        t = x.astype(BF16).astype(F32)
        terms.append(t)
        x = x - t
    terms.append(x)
    return terms


def _moba_prepare_kernel(k_ref, v_ref, km_ref, ka_ref, vt_ref):
    nblk = km_ref.shape[1]
    bs = MOBA_BLOCK
    seq, kvw = k_ref.shape
    k = k_ref[...]
    km_ref[0] = jnp.mean(k.reshape(nblk, bs, kvw), axis=1)
    col = lax.broadcasted_iota(jnp.int32, (seq, MOBA_EXTRA), 1)
    row = lax.broadcasted_iota(jnp.int32, (seq, MOBA_EXTRA), 0)
    in_onehot = col < MOBA_SPLIT * nblk
    in_offset = (col >= MOBA_SPLIT * nblk) & (col < MOBA_SPLIT * (nblk + 1))
    onehot = jnp.where(in_onehot & (col % nblk == row // bs), 1.0, 0.0)
    extra = jnp.where(in_offset, (row % bs).astype(F32), onehot)
    for kv in range(kvw // HEAD_DIM):
        ka_ref[0, kv] = jnp.concatenate([k[:, kv * HEAD_DIM:(kv + 1) * HEAD_DIM], extra], axis=1).astype(BF16)
    for j in range(nblk):
        vt_ref[0, j] = v_ref[j * bs:(j + 1) * bs, :].T.astype(BF16)


def _moba_prepare(proj, batch, seq, *, k_col, v_col, kvw):
    nblk = seq // MOBA_BLOCK
    kv_heads = kvw // HEAD_DIM
    return pl.pallas_call(
        _moba_prepare_kernel,
        grid=(batch,),
        in_specs=[pl.BlockSpec((seq, kvw), lambda b: (b, k_col)),
                  pl.BlockSpec((seq, kvw), lambda b: (b, v_col))],
        out_specs=(pl.BlockSpec((1, nblk, kvw), lambda b: (b, 0, 0)),
                   pl.BlockSpec((1, kv_heads, seq, HEAD_DIM + MOBA_EXTRA), lambda b: (b, 0, 0, 0)),
                   pl.BlockSpec((1, nblk, kvw, MOBA_BLOCK), lambda b: (b, 0, 0, 0))),
        out_shape=(jax.ShapeDtypeStruct((batch, nblk, kvw), F32),
                   jax.ShapeDtypeStruct((batch, kv_heads, seq, HEAD_DIM + MOBA_EXTRA), BF16),
                   jax.ShapeDtypeStruct((batch, nblk, kvw, MOBA_BLOCK), BF16)),
        compiler_params=_params("arbitrary"),
        name="moba_prepare",
    )(proj, proj)


def _moba_kernel(q_ref, ka_ref, vt_ref, km_ref, o_ref, qa_scr, acc_scr, m_scr, l_scr, ot_scr, *, slopes):
    n = pl.program_id(1)
    bs = MOBA_BLOCK
    nblk = km_ref.shape[1]
    kv_heads = ka_ref.shape[1]
    cols = GQA_GROUP * bs

    lane = lax.broadcasted_iota(jnp.int32, (1, cols), 1)
    head_of_lane = lane // bs
    blk_t = lax.broadcasted_iota(jnp.int32, (nblk, cols), 0)
    dist0 = ((n - blk_t) * bs + (lane & (bs - 1))).astype(F32)
    sub = lax.broadcasted_iota(jnp.int32, (SUBLANES, cols), 0)
    pad = jnp.zeros((MOBA_EXTRA - MOBA_SPLIT * nblk - SUBLANES, cols), F32)

    q_t = q_ref[...].T

    for kv in range(kv_heads):
        slope = jnp.zeros((1, cols), F32)
        for g in range(GQA_GROUP):
            slope = jnp.where(head_of_lane == g, slopes[kv * GQA_GROUP + g], slope)
        heads = [q_t[(kv * GQA_GROUP + g) * HEAD_DIM:(kv * GQA_GROUP + g + 1) * HEAD_DIM, :]
                 for g in range(GQA_GROUP)]
        qf_t = jnp.concatenate(heads, axis=1) * HEAD_DIM ** -0.5

        gate_t = jnp.dot(km_ref[0][:, kv * HEAD_DIM:(kv + 1) * HEAD_DIM], qf_t,
                         preferred_element_type=F32, precision=lax.Precision.HIGHEST)
        rank = jnp.zeros((nblk, cols), jnp.int32)
        for c in range(nblk):
            gc = gate_t[c:c + 1, :]
            beats = (gc > gate_t) | ((gc == gate_t) & (c < blk_t))
            rank = rank + jnp.where(beats, jnp.where(c < n, 1, 0), 0)
        chosen = ((blk_t < n) & (rank < MOBA_TOPK)) | (blk_t == n)
        bias_t = jnp.where(chosen, 0.0, NEG_INF) - slope * dist0
        slope_rows = jnp.zeros((SUBLANES, cols), F32)
        for t, term in enumerate(_split_bf16(slope)):
            slope_rows = jnp.where(sub == t, term, slope_rows)
        qa_scr[kv] = jnp.concatenate([qf_t] + _split_bf16(bias_t) + [slope_rows, pad], axis=0).astype(BF16)

    m_scr[...] = jnp.full_like(m_scr, NEG_INF)
    l_scr[...] = jnp.zeros_like(l_scr)
    acc_scr[...] = jnp.zeros_like(acc_scr)

    def attend(j0, nb, mask):
        start = pl.multiple_of(j0 * bs, bs)
        for kv in range(kv_heads):
            s = jnp.dot(ka_ref[0, kv, pl.ds(start, nb * bs), :], qa_scr[kv], preferred_element_type=F32)
            if mask is not None:
                s = jnp.where(mask, s, NEG_INF)
            m_old = m_scr[kv]
            m_new = jnp.maximum(m_old, jnp.max(s, axis=0, keepdims=True))
            alpha = jnp.exp(m_old - m_new)
            p = jnp.exp(s - m_new)
            v_t = jnp.concatenate([vt_ref[0, j0 + t, kv * HEAD_DIM:(kv + 1) * HEAD_DIM, :] for t in range(nb)],
                                  axis=1)
            acc_scr[kv] = alpha * acc_scr[kv] + jnp.dot(v_t, p.astype(BF16), preferred_element_type=F32)
            l_scr[kv] = alpha * l_scr[kv] + jnp.sum(p, axis=0, keepdims=True)
            m_scr[kv] = m_new

    tok = lax.broadcasted_iota(jnp.int32, (1, cols), 1) & (bs - 1)
    odd = (n + 1) % 2

    @pl.when(odd == 1)
    def _():
        visible = jnp.where(n > 0, bs, 0)
        key1 = lax.broadcasted_iota(jnp.int32, (bs, cols), 0)
        attend(0, 1, tok + visible >= key1)

    def body(i, carry):
        attend(odd + 2 * i, 2, None)
        return carry

    lax.fori_loop(0, (n + 1) // 2 - 1, body, 0)

    @pl.when(n >= 1)
    def _():
        key2 = lax.broadcasted_iota(jnp.int32, (2 * bs, cols), 0)
        attend(n - 1, 2, tok + bs >= key2)

    for kv in range(kv_heads):
        o_t = acc_scr[kv] / l_scr[kv]
        for g in range(GQA_GROUP):
            h = kv * GQA_GROUP + g
            ot_scr[h * HEAD_DIM:(h + 1) * HEAD_DIM, :] = o_t[:, g * bs:(g + 1) * bs]
    o_ref[...] = ot_scr[...].T


def _moba_attention(proj, batch, seq, *, width, q_col, k_col, v_col):
    n = proj.shape[0]
    kvw = width // GQA_GROUP
    kv_heads = kvw // HEAD_DIM
    bs = MOBA_BLOCK
    nblk = seq // bs
    assert MOBA_SPLIT * nblk + SUBLANES <= MOBA_EXTRA
    k_means, k_aug, v_t = _moba_prepare(proj, batch, seq, k_col=k_col, v_col=v_col, kvw=kvw)
    cols = GQA_GROUP * bs
    return pl.pallas_call(
        functools.partial(_moba_kernel, slopes=_alibi_slopes(width // HEAD_DIM)),
        grid=(batch, nblk),
        in_specs=[
            pl.BlockSpec((bs, width), lambda b, i: (b * nblk + i, q_col)),
            pl.BlockSpec((1, kv_heads, seq, HEAD_DIM + MOBA_EXTRA), lambda b, i: (b, 0, 0, 0)),
            pl.BlockSpec((1, nblk, kvw, bs), lambda b, i: (b, 0, 0, 0)),
            pl.BlockSpec((1, nblk, kvw), lambda b, i: (b, 0, 0)),
        ],
        out_specs=pl.BlockSpec((bs, width), lambda b, i: (b * nblk + i, 0)),
        out_shape=jax.ShapeDtypeStruct((n, width), F32),
        scratch_shapes=[
            pltpu.VMEM((kv_heads, HEAD_DIM + MOBA_EXTRA, cols), BF16),
            pltpu.VMEM((kv_heads, HEAD_DIM, cols), F32),
            pltpu.VMEM((kv_heads, 1, cols), F32),
            pltpu.VMEM((kv_heads, 1, cols), F32),
            pltpu.VMEM((width, bs), F32),
        ],
        compiler_params=_params("arbitrary", "arbitrary"),
        name="moba_attention",
    )(proj, k_aug, v_t, k_means)


def kernel(x, c, ada_w, ada_b, norm_mix_pre, norm_mix_post, norm_ffn_pre, norm_ffn_post, ffn_w_up, ffn_conv_w, ffn_conv_b, ffn_w_down, ev_w_in, ev_w_out, ev_sinks, ev_pool_w, ev_pool_b, ev_pool_scale, od_w_in, od_w_out, od_conv_w, od_conv_b, od_gate_b, od_mh_norm):
    batch, seq, d = x.shape
    depth = ada_w.shape[0]
    assert seq % 512 == 0
    x2 = x.reshape(batch * seq, d)

    mod = _ada_modulation(c, ada_w, ada_b)
    mod = mod.reshape(depth, batch, 6, 1, d)

    for layer in range(depth):
        sh_m, sc_m, gt_m, sh_f, sc_f, gt_f = [mod[layer, :, r] for r in range(6)]
        j = layer // 2
        if layer % 2 == 0:
            width = ev_pool_scale.shape[1]
            kvw = width // GQA_GROUP
            w_in = ev_w_in[j]
            w_in = jnp.concatenate([w_in[:, :width], w_in[:, width + 2 * kvw:],
                                    w_in[:, width:width + 2 * kvw]], axis=1).astype(BF16)
            proj = _input_projection(x2, seq, norm_mix_pre[layer], sc_m, sh_m, w_in, tm=1024, tn=1280)
            a = _sliding_window_attention(proj, ev_sinks[j], batch, seq, width=width,
                                          k_col=2 * width // kvw, v_col=2 * width // kvw + 1, tq=512)
            b = _pool_mixer(proj, ev_pool_w[j].astype(BF16), ev_pool_b[j], ev_pool_scale[j], batch, seq,
                            col=1, tt=512)
            w_out = ev_w_out[j].astype(BF16)
        else:
            width = od_mh_norm.shape[1]
            kvw = width // GQA_GROUP
            heads = MLSTM_HEADS
            w_in = od_w_in[j]
            g0 = 3 * width
            w_main = jnp.concatenate([w_in[:, :g0], w_in[:, g0 + 2 * heads:]], axis=1).astype(BF16)
            w_gates = jnp.zeros((d, LANES), BF16).at[:, :2 * heads].set(
                w_in[:, g0:g0 + 2 * heads].astype(BF16))
            proj, gates = _input_projection(x2, seq, norm_mix_pre[layer], sc_m, sh_m, w_main, w_gates,
                                            tm=1024, tn=1536)
            a = _mlstm(proj, gates, od_conv_w[j], od_conv_b[j], od_gate_b[j], od_mh_norm[j], batch, seq)
            b = _moba_attention(proj, batch, seq, width=width, q_col=3,
                                k_col=4 * width // kvw, v_col=4 * width // kvw + 1)
            w_out = od_w_out[j].astype(BF16)
        x2 = _output_projection(a, b, w_out, x2, seq, gt_m, norm_mix_post[layer], tm=512)
        x2 = _conv_ffn(x2, seq, norm_ffn_pre[layer], sc_f, sh_f, ffn_w_up[layer].astype(BF16),
                       ffn_conv_w[layer], ffn_conv_b[layer], ffn_w_down[layer].astype(BF16),
                       gt_f, norm_ffn_post[layer], tm=1024, tf=512)
    return x2.reshape(batch, seq, d)
```

```python
import functools
import math

import jax
import jax.numpy as jnp
from jax import lax
from jax.experimental import pallas as pl
from jax.experimental.pallas import tpu as pltpu

F32 = jnp.float32
BF16 = jnp.bfloat16

HEAD_DIM = 64
GQA_GROUP = 4
SWA_WINDOW = 128
POOL_WINDOWS = (2, 4, 8, 16)
POOL_HALO = 16
MLSTM_HEADS = 4
MLSTM_CONV = 4
MLSTM_CHUNK = 256
MOBA_BLOCK = 256
MOBA_TOPK = 3
FFN_CONV = 3
FFN_CHUNK = 256
NORM_EPS = 1e-6
NEG_INF = -1e30
LANES = 128
SUBLANES = 8
VMEM_LIMIT = 60 * 1024 * 1024

NT = (((1,), (1,)), ((), ()))


def _alibi_slopes(n_heads):
    return [2.0 ** (-8.0 * (h + 1) / n_heads) for h in range(n_heads)]


def _params(*semantics):
    return pltpu.CompilerParams(dimension_semantics=semantics, vmem_limit_bytes=VMEM_LIMIT)


def _rms(x, gain):
    ms = jnp.mean(x * x, axis=-1, keepdims=True)
    return x * lax.rsqrt(ms + NORM_EPS) * gain


def _sigmoid(x):
    return 1.0 / (1.0 + jnp.exp(-x))


def _log_sigmoid(x):
    return jnp.minimum(x, 0.0) - jnp.log(1.0 + jnp.exp(-jnp.abs(x)))


def _ada_kernel(c_ref, w_ref, b_ref, o_ref):
    c = c_ref[...]
    a = (c * _sigmoid(c)).astype(BF16)
    o_ref[0] = jnp.dot(a, w_ref[0].astype(BF16), preferred_element_type=F32) + b_ref[0]


def _ada_modulation(c, ada_w, ada_b):
    depth, d, n_out = ada_w.shape
    rows = 2 * SUBLANES
    c_pad = jnp.zeros((rows, d), F32).at[:c.shape[0]].set(c)
    tn = 1024
    out = pl.pallas_call(
        _ada_kernel,
        grid=(depth, n_out // tn),
        in_specs=[
            pl.BlockSpec((rows, d), lambda l, j: (0, 0)),
            pl.BlockSpec((1, d, tn), lambda l, j: (l, 0, j)),
            pl.BlockSpec((1, 1, tn), lambda l, j: (l, 0, j)),
        ],
        out_specs=pl.BlockSpec((1, rows, tn), lambda l, j: (l, 0, j)),
        out_shape=jax.ShapeDtypeStruct((depth, rows, n_out), F32),
        compiler_params=_params("arbitrary", "arbitrary"),
        name="ada_modulation",
    )(c_pad, ada_w, ada_b.reshape(depth, 1, n_out))
    return out[:, :c.shape[0]]


def _modulated_norm(x_ref, g_ref, sc_ref, sh_ref):
    return (_rms(x_ref[...], g_ref[...]) * (1.0 + sc_ref[0]) + sh_ref[0]).astype(BF16)


def _inproj_kernel(x_ref, g_ref, sc_ref, sh_ref, w_ref, o_ref, h_scr):
    @pl.when(pl.program_id(1) == 0)
    def _():
        h_scr[...] = _modulated_norm(x_ref, g_ref, sc_ref, sh_ref)

    o_ref[...] = jnp.dot(h_scr[...], w_ref[...], preferred_element_type=F32)


def _inproj_gates_kernel(x_ref, g_ref, sc_ref, sh_ref, w_ref, wg_ref, o_ref, og_ref, h_scr):
    @pl.when(pl.program_id(1) == 0)
    def _():
        h = _modulated_norm(x_ref, g_ref, sc_ref, sh_ref)
        h_scr[...] = h
        og_ref[...] = jnp.dot(h, wg_ref[...], preferred_element_type=F32)

    o_ref[...] = jnp.dot(h_scr[...], w_ref[...], preferred_element_type=F32)


def _input_projection(x2, seq, gain, scale, shift, w, w_gates=None, *, tm, tn):
    n, d = x2.shape
    n_out = w.shape[1]
    per_seq = seq // tm
    row = lambda i, j: (i, 0)
    mod = lambda i, j: (i // per_seq, 0, 0)
    in_specs = [
        pl.BlockSpec((tm, d), row),
        pl.BlockSpec((1, d), lambda i, j: (0, 0)),
        pl.BlockSpec((1, 1, d), mod),
        pl.BlockSpec((1, 1, d), mod),
        pl.BlockSpec((d, tn), lambda i, j: (0, j)),
    ]
    args = [x2, gain.reshape(1, d), scale, shift, w]
    out_specs = pl.BlockSpec((tm, tn), lambda i, j: (i, j))
    out_shape = jax.ShapeDtypeStruct((n, n_out), F32)
    body = _inproj_kernel
    if w_gates is not None:
        in_specs.append(pl.BlockSpec((d, LANES), lambda i, j: (0, 0)))
        args.append(w_gates)
        out_specs = (out_specs, pl.BlockSpec((tm, LANES), row))
        out_shape = (out_shape, jax.ShapeDtypeStruct((n, LANES), F32))
        body = _inproj_gates_kernel
    return pl.pallas_call(
        body,
        grid=(n // tm, n_out // tn),
        in_specs=in_specs,
        out_specs=out_specs,
        out_shape=out_shape,
        scratch_shapes=[pltpu.VMEM((tm, d), BF16)],
        compiler_params=_params("arbitrary", "arbitrary"),
        name="input_projection",
    )(*args)


def _swa_kernel(sinks_ref, q_ref, kc_ref, vc_ref, kp_ref, vp_ref, o_ref, ot_scr, *, slopes):
    first_tile = pl.program_id(1) == 0
    w = SWA_WINDOW
    cols = GQA_GROUP * w
    kv_heads = kc_ref.shape[1] // HEAD_DIM
    key = lax.broadcasted_iota(jnp.int32, (2 * w, cols), 0)
    lane = lax.broadcasted_iota(jnp.int32, (2 * w, cols), 1)
    dist = (lane & (w - 1)) + w - key
    band = (dist >= 0) & (dist < w)
    distf = dist.astype(F32)
    head_of_lane = lane // w
    head_row = lax.broadcasted_iota(jnp.int32, (1, cols), 1) // w
    first_key = jnp.where(first_tile, w, 0)

    q_t = q_ref[...].T
    v_t = vc_ref[...].T
    vp_t = vp_ref[...].T

    for kv in range(kv_heads):
        slope = jnp.zeros((2 * w, cols), F32)
        sink = jnp.zeros((1, cols), F32)
        for g in range(GQA_GROUP):
            h = kv * GQA_GROUP + g
            slope = jnp.where(head_of_lane == g, slopes[h], slope)
            sink = jnp.where(head_row == g, sinks_ref[h], sink)
        table = jnp.where(band, -slope * distf, NEG_INF)
        lanes = slice(kv * HEAD_DIM, (kv + 1) * HEAD_DIM)
        for sb in range(q_ref.shape[0] // w):
            toks = slice(sb * w, (sb + 1) * w)
            q = jnp.concatenate(
                [q_t[(kv * GQA_GROUP + g) * HEAD_DIM:(kv * GQA_GROUP + g + 1) * HEAD_DIM, toks]
                 for g in range(GQA_GROUP)], axis=1) * HEAD_DIM ** -0.5
            if sb == 0:
                k2 = jnp.concatenate([kp_ref[:, lanes], kc_ref[0:w, lanes]], axis=0)
                v2 = jnp.concatenate([vp_t[lanes, :], v_t[lanes, 0:w]], axis=1)
            else:
                k2 = kc_ref[(sb - 1) * w:(sb + 1) * w, lanes]
                v2 = v_t[lanes, (sb - 1) * w:(sb + 1) * w]
            s = jnp.dot(k2.astype(BF16), q.astype(BF16), preferred_element_type=F32)
            logits = s + table
            if sb == 0:
                logits = jnp.where(key >= first_key, logits, NEG_INF)
            m = jnp.maximum(jnp.max(logits, axis=0, keepdims=True), sink)
            e = jnp.exp(logits - m)
            denom = jnp.sum(e, axis=0, keepdims=True) + jnp.exp(sink - m)
            o = jnp.dot(v2.astype(BF16), e.astype(BF16), preferred_element_type=F32) / denom
            for g in range(GQA_GROUP):
                h = kv * GQA_GROUP + g
                ot_scr[h * HEAD_DIM:(h + 1) * HEAD_DIM, toks] = o[:, g * w:(g + 1) * w]
    o_ref[...] = ot_scr[...].T


def _sliding_window_attention(proj, sinks, batch, seq, *, width, k_col, v_col, tq):
    n = proj.shape[0]
    kvw = width // GQA_GROUP
    per_seq = seq // tq
    sub = tq // SWA_WINDOW
    cur = lambda col: (lambda b, i: (b * per_seq + i, col))
    prev = lambda col: (lambda b, i: (jnp.maximum((b * per_seq + i) * sub - 1, 0), col))
    return pl.pallas_call(
        functools.partial(_swa_kernel, slopes=_alibi_slopes(width // HEAD_DIM)),
        grid=(batch, per_seq),
        in_specs=[
            pl.BlockSpec(memory_space=pltpu.SMEM),
            pl.BlockSpec((tq, width), cur(0)),
            pl.BlockSpec((tq, kvw), cur(k_col)),
            pl.BlockSpec((tq, kvw), cur(v_col)),
            pl.BlockSpec((SWA_WINDOW, kvw), prev(k_col)),
            pl.BlockSpec((SWA_WINDOW, kvw), prev(v_col)),
        ],
        out_specs=pl.BlockSpec((tq, width), cur(0)),
        out_shape=jax.ShapeDtypeStruct((n, width), F32),
        scratch_shapes=[pltpu.VMEM((width, tq), F32)],
        compiler_params=_params("arbitrary", "arbitrary"),
        name="sliding_window_attention",
    )(sinks, proj, proj, proj, proj, proj)


def _pool_kernel(p_ref, halo_ref, w_ref, b_ref, sc_ref, o_ref):
    tt = p_ref.shape[0]
    gw = w_ref.shape[1]
    i = pl.program_id(1)
    pos1 = (i * tt + 1 + lax.broadcasted_iota(jnp.int32, (tt, 1), 0)).astype(F32)
    for g, win in enumerate(POOL_WINDOWS):
        lanes = slice(g * gw, (g + 1) * gw)
        cur = p_ref[:, lanes]
        halo = jnp.where(i > 0, halo_ref[:, lanes], 0.0)
        a = jnp.concatenate([halo, cur], axis=0)
        step = 1
        while step < win:
            a = a[step:] + a[:-step]
            step *= 2
        off = POOL_HALO - (win - 1)
        mean = a[off:off + tt] / jnp.minimum(pos1, float(win))
        d = (mean - cur).astype(BF16)
        y = jnp.dot(d, w_ref[g], preferred_element_type=F32) + b_ref[g]
        o_ref[:, lanes] = y * sc_ref[:, lanes]


def _pool_mixer(proj, pool_w, pool_b, pool_scale, batch, seq, *, col, tt):
    n = proj.shape[0]
    groups, gw, _ = pool_w.shape
    width = groups * gw
    per_seq = seq // tt
    halo_blocks = tt // POOL_HALO
    return pl.pallas_call(
        _pool_kernel,
        grid=(batch, per_seq),
        in_specs=[
            pl.BlockSpec((tt, width), lambda b, i: (b * per_seq + i, col)),
            pl.BlockSpec((POOL_HALO, width),
                         lambda b, i: (jnp.maximum((b * per_seq + i) * halo_blocks - 1, 0), col)),
            pl.BlockSpec((groups, gw, gw), lambda b, i: (0, 0, 0)),
            pl.BlockSpec((groups, 1, gw), lambda b, i: (0, 0, 0)),
            pl.BlockSpec((1, width), lambda b, i: (0, 0)),
        ],
        out_specs=pl.BlockSpec((tt, width), lambda b, i: (b * per_seq + i, 0)),
        out_shape=jax.ShapeDtypeStruct((n, width), F32),
        compiler_params=_params("arbitrary", "arbitrary"),
        name="pool_mixer",
    )(proj, proj, pool_w, pool_b.reshape(groups, 1, gw), pool_scale.reshape(1, width))


def _outproj_kernel(a_ref, b_ref, w_ref, x_ref, gate_ref, g_ref, o_ref):
    half = a_ref.shape[1]
    y = jnp.dot(a_ref[...].astype(BF16), w_ref[0:half, :], preferred_element_type=F32)
    y = y + jnp.dot(b_ref[...].astype(BF16), w_ref[half:, :], preferred_element_type=F32)
    o_ref[...] = x_ref[...] + gate_ref[0] * _rms(y, g_ref[...])


def _output_projection(a, b, w, x2, seq, gate, gain, *, tm):
    n, d = x2.shape
    half = a.shape[1]
    per_seq = seq // tm
    row = lambda i: (i, 0)
    return pl.pallas_call(
        _outproj_kernel,
        grid=(n // tm,),
        in_specs=[
            pl.BlockSpec((tm, half), row),
            pl.BlockSpec((tm, half), row),
            pl.BlockSpec((2 * half, d), lambda i: (0, 0)),
            pl.BlockSpec((tm, d), row),
            pl.BlockSpec((1, 1, d), lambda i: (i // per_seq, 0, 0)),
            pl.BlockSpec((1, d), lambda i: (0, 0)),
        ],
        out_specs=pl.BlockSpec((tm, d), row),
        out_shape=jax.ShapeDtypeStruct((n, d), F32),
        compiler_params=_params("arbitrary"),
        name="output_projection",
    )(a, b, w, x2, gate, gain.reshape(1, d))


def _ffn_kernel(x_ref, g_ref, sc_ref, sh_ref, wu_ref, wg_ref, cw_ref, cb_ref, wd_ref,
                gate_ref, gpost_ref, o_ref, h_scr, act_scr, carry_scr, *, per_seq):
    i = pl.program_id(0)
    j = pl.program_id(1)
    tm, d = x_ref.shape
    tf = wd_ref.shape[0]

    @pl.when(j == 0)
    def _():
        h_scr[...] = _modulated_norm(x_ref, g_ref, sc_ref, sh_ref)
        o_ref[...] = jnp.zeros_like(o_ref)

    @pl.when((j == 0) & (i % per_seq == 0))
    def _():
        carry_scr[...] = jnp.zeros_like(carry_scr)

    h = h_scr[...]
    row = lax.broadcasted_iota(jnp.int32, (tm, FFN_CHUNK), 0)
    for c0 in range(0, tf, FFN_CHUNK):
        cols = slice(c0, c0 + FFN_CHUNK)
        u = jnp.dot(h, wu_ref[:, cols], preferred_element_type=F32)
        g = jnp.dot(h, wg_ref[:, cols], preferred_element_type=F32)
        tail = carry_scr[j, :, cols]
        carry_scr[j, :, cols] = g[tm - SUBLANES:, :]
        g1 = jnp.where(row == 0, tail[SUBLANES - 1:, :], pltpu.roll(g, 1, 0))
        g2 = jnp.where(row == 0, tail[SUBLANES - 2:SUBLANES - 1, :],
                       jnp.where(row == 1, tail[SUBLANES - 1:, :], pltpu.roll(g, 2, 0)))
        cw = cw_ref[:, cols]
        gc = cb_ref[:, cols] + cw[0:1] * g2 + cw[1:2] * g1 + cw[2:3] * g
        act = 0.5 * gc * (1.0 + jnp.tanh(math.sqrt(2.0 / math.pi) * (gc + 0.044715 * (gc * gc * gc))))
        act_scr[:, cols] = (act * u).astype(BF16)

    act = act_scr[...]
    for n0 in range(0, d, d // 2):
        out_cols = slice(n0, n0 + d // 2)
        o_ref[:, out_cols] += jnp.dot(act, wd_ref[:, out_cols], preferred_element_type=F32)

    @pl.when(j == pl.num_programs(1) - 1)
    def _():
        o_ref[...] = x_ref[...] + gate_ref[0] * _rms(o_ref[...], gpost_ref[...])


def _conv_ffn(x2, seq, gain, scale, shift, w_up, conv_w, conv_b, w_down, gate, gain_post, *, tm, tf):
    n, d = x2.shape
    d_ff = w_down.shape[0]
    nf = d_ff // tf
    per_seq = seq // tm
    row = lambda i, j: (i, 0)
    mod = lambda i, j: (i // per_seq, 0, 0)
    const = lambda i, j: (0, 0)
    return pl.pallas_call(
        functools.partial(_ffn_kernel, per_seq=per_seq),
        grid=(n // tm, nf),
        in_specs=[
            pl.BlockSpec((tm, d), row, pipeline_mode=pl.Buffered(1)),
            pl.BlockSpec((1, d), const),
            pl.BlockSpec((1, 1, d), mod),
            pl.BlockSpec((1, 1, d), mod),
            pl.BlockSpec((d, tf), lambda i, j: (0, j)),
            pl.BlockSpec((d, tf), lambda i, j: (0, nf + j)),
            pl.BlockSpec((FFN_CONV, tf), lambda i, j: (0, j)),
            pl.BlockSpec((1, tf), lambda i, j: (0, j)),
            pl.BlockSpec((tf, d), lambda i, j: (j, 0)),
            pl.BlockSpec((1, 1, d), mod),
            pl.BlockSpec((1, d), const),
        ],
        out_specs=pl.BlockSpec((tm, d), row),
        out_shape=jax.ShapeDtypeStruct((n, d), F32),
        scratch_shapes=[
            pltpu.VMEM((tm, d), BF16),
            pltpu.VMEM((tm, tf), BF16),
            pltpu.VMEM((nf, SUBLANES, tf), F32),
        ],
        compiler_params=_params("arbitrary", "arbitrary"),
        name="conv_ffn",
    )(x2, gain.reshape(1, d), scale, shift, w_up, w_up, conv_w, conv_b.reshape(1, d_ff), w_down,
      gate, gain_post.reshape(1, d))


def _mlstm_kernel(qk_ref, v_ref, o_ref, gt_ref, cw_ref, cb_ref, gb_ref, mh_ref, out_ref,
                  tail_scr, c_scr, n_scr, m_scr):
    L = qk_ref.shape[0]
    heads = MLSTM_HEADS
    dqk = qk_ref.shape[1] // (2 * heads)
    dv = v_ref.shape[1] // heads

    @pl.when(pl.program_id(1) == 0)
    def _():
        tail_scr[...] = jnp.zeros_like(tail_scr)
        c_scr[...] = jnp.zeros_like(c_scr)
        n_scr[...] = jnp.zeros_like(n_scr)
        m_scr[...] = jnp.zeros_like(m_scr)

    raw = qk_ref[...]
    tail = tail_scr[...]
    tail_scr[...] = raw[L - SUBLANES:, :]
    row = lax.broadcasted_iota(jnp.int32, raw.shape, 0)
    cw = cw_ref[...]
    y = cb_ref[...] + cw[MLSTM_CONV - 1:MLSTM_CONV] * raw
    for lag in range(1, MLSTM_CONV):
        shifted = pltpu.roll(raw, lag, 0)
        for r in range(lag):
            shifted = jnp.where(row == r, tail[SUBLANES - lag + r:SUBLANES - lag + r + 1, :], shifted)
        y = y + cw[MLSTM_CONV - 1 - lag:MLSTM_CONV - lag] * shifted
    qk = y * _sigmoid(y)
    q_all = qk[:, :heads * dqk]
    k_all = qk[:, heads * dqk:] * (dqk ** -0.5)

    gates = gt_ref[...] + gb_ref[...]
    lf = _log_sigmoid(gates)
    r_i = lax.broadcasted_iota(jnp.int32, (L, L), 0)
    c_i = lax.broadcasted_iota(jnp.int32, (L, L), 1)
    causal = r_i >= c_i
    tri = causal.astype(F32)
    b_cols = jnp.dot(tri, lf, preferred_element_type=F32, precision=lax.Precision.HIGHEST)
    gates_t = gates.T
    lf_t = _log_sigmoid(gates_t)
    b_rows = lax.dot_general(lf_t, tri, NT, preferred_element_type=F32, precision=lax.Precision.HIGHEST)

    for h in range(heads):
        i_col = gates[:, h:h + 1]
        i_row = gates_t[h:h + 1, :]
        b_col = b_cols[:, heads + h:heads + h + 1]
        b_row = b_rows[heads + h:heads + h + 1, :]
        m_prev = m_scr[h][:, 0:1]
        n_prev = n_scr[h]
        c_prev = c_scr[h]
        qh = q_all[:, h * dqk:(h + 1) * dqk]
        kh = k_all[:, h * dqk:(h + 1) * dqk]
        vh = v_ref[:, h * dv:(h + 1) * dv].astype(BF16)
        qb = qh.astype(BF16)

        logd = jnp.where(causal, b_col - b_row + i_row, -jnp.inf)
        m_inter = m_prev + b_col
        m_row = jnp.maximum(m_inter, jnp.max(logd, axis=-1, keepdims=True))
        w_inter = jnp.exp(m_inter - m_row)
        s = lax.dot_general(qb, kh.astype(BF16), NT, preferred_element_type=F32) * jnp.exp(logd - m_row)
        num = (w_inter * jnp.dot(qb, c_prev.astype(BF16), preferred_element_type=F32)
               + jnp.dot(s.astype(BF16), vh, preferred_element_type=F32))
        den = w_inter * jnp.sum(qh * n_prev, axis=-1, keepdims=True) + jnp.sum(s, axis=-1, keepdims=True)
        hh = num / jnp.maximum(jnp.abs(den), jnp.exp(-m_row))

        b_last = b_col[L - 1:, :]
        logw = b_last - b_col + i_col
        m_new = jnp.maximum(m_prev + b_last, jnp.max(logw, axis=0, keepdims=True))
        w_old = jnp.exp(m_prev + b_last - m_new)
        wk = jnp.exp(logw - m_new) * kh
        c_scr[h] = w_old * c_prev + jnp.dot(wk.T.astype(BF16), vh, preferred_element_type=F32)
        n_scr[h] = w_old * n_prev + jnp.sum(wk, axis=0, keepdims=True)
        m_scr[h] = jnp.broadcast_to(m_new, (1, LANES))

        cols = slice(h * dv, (h + 1) * dv)
        out_ref[:, cols] = _rms(hh, mh_ref[:, cols]) * _sigmoid(o_ref[:, cols])


def _mlstm(proj, gates, conv_w, conv_b, gate_b, mh_norm, batch, seq):
    n = proj.shape[0]
    width = mh_norm.shape[0]
    heads = MLSTM_HEADS
    dqk = width // heads // 2
    L = MLSTM_CHUNK
    per_seq = seq // L
    blk = lambda col: (lambda b, c: (b * per_seq + c, col))
    const = lambda b, c: (0, 0)
    gate_b_pad = jnp.zeros((1, LANES), F32).at[0, :2 * heads].set(gate_b)
    return pl.pallas_call(
        _mlstm_kernel,
        grid=(batch, per_seq),
        in_specs=[
            pl.BlockSpec((L, width), blk(0)),
            pl.BlockSpec((L, width), blk(1)),
            pl.BlockSpec((L, width), blk(2)),
            pl.BlockSpec((L, LANES), blk(0)),
            pl.BlockSpec((MLSTM_CONV, width), const),
            pl.BlockSpec((1, width), const),
            pl.BlockSpec((1, LANES), const),
            pl.BlockSpec((1, width), const),
        ],
        out_specs=pl.BlockSpec((L, width), blk(0)),
        out_shape=jax.ShapeDtypeStruct((n, width), F32),
        scratch_shapes=[
            pltpu.VMEM((SUBLANES, width), F32),
            pltpu.VMEM((heads, dqk, width // heads), F32),
            pltpu.VMEM((heads, 1, dqk), F32),
            pltpu.VMEM((heads, 1, LANES), F32),
        ],
        compiler_params=_params("arbitrary", "arbitrary"),
        name="mlstm",
    )(proj, proj, proj, gates, conv_w, conv_b.reshape(1, width), gate_b_pad, mh_norm.reshape(1, width))


MOBA_SPLIT = 3
MOBA_EXTRA = HEAD_DIM


def _split_bf16(x):
    terms = []
    for _ in range(MOBA_SPLIT - 1):
        t = x.astype(BF16).astype(F32)
        terms.append(t)
        x = x - t
    terms.append(x)
    return terms


def _moba_prepare_kernel(k_ref, v_ref, km_ref, ka_ref, vt_ref):
    nblk = km_ref.shape[1]
    bs = MOBA_BLOCK
    seq, kvw = k_ref.shape
    k = k_ref[...]
    km_ref[0] = jnp.mean(k.reshape(nblk, bs, kvw), axis=1)
    col = lax.broadcasted_iota(jnp.int32, (seq, MOBA_EXTRA), 1)
    row = lax.broadcasted_iota(jnp.int32, (seq, MOBA_EXTRA), 0)
    in_onehot = col < MOBA_SPLIT * nblk
    in_offset = (col >= MOBA_SPLIT * nblk) & (col < MOBA_SPLIT * (nblk + 1))
    onehot = jnp.where(in_onehot & (col % nblk == row // bs), 1.0, 0.0)
    extra = jnp.where(in_offset, (row % bs).astype(F32), onehot)
    for kv in range(kvw // HEAD_DIM):
        ka_ref[0, kv] = jnp.concatenate([k[:, kv * HEAD_DIM:(kv + 1) * HEAD_DIM], extra], axis=1).astype(BF16)
    for j in range(nblk):
        vt_ref[0, j] = v_ref[j * bs:(j + 1) * bs, :].T.astype(BF16)


def _moba_prepare(proj, batch, seq, *, k_col, v_col, kvw):
    nblk = seq // MOBA_BLOCK
    kv_heads = kvw // HEAD_DIM
    return pl.pallas_call(
        _moba_prepare_kernel,
        grid=(batch,),
        in_specs=[pl.BlockSpec((seq, kvw), lambda b: (b, k_col)),
                  pl.BlockSpec((seq, kvw), lambda b: (b, v_col))],
        out_specs=(pl.BlockSpec((1, nblk, kvw), lambda b: (b, 0, 0)),
                   pl.BlockSpec((1, kv_heads, seq, HEAD_DIM + MOBA_EXTRA), lambda b: (b, 0, 0, 0)),
                   pl.BlockSpec((1, nblk, kvw, MOBA_BLOCK), lambda b: (b, 0, 0, 0))),
        out_shape=(jax.ShapeDtypeStruct((batch, nblk, kvw), F32),
                   jax.ShapeDtypeStruct((batch, kv_heads, seq, HEAD_DIM + MOBA_EXTRA), BF16),
                   jax.ShapeDtypeStruct((batch, nblk, kvw, MOBA_BLOCK), BF16)),
        compiler_params=_params("arbitrary"),
        name="moba_prepare",
    )(proj, proj)


def _moba_kernel(q_ref, ka_ref, vt_ref, km_ref, o_ref, qa_scr, acc_scr, m_scr, l_scr, ot_scr, *, slopes):
    n = pl.program_id(1)
    bs = MOBA_BLOCK
    nblk = km_ref.shape[1]
    kv_heads = ka_ref.shape[1]
    cols = GQA_GROUP * bs

    lane = lax.broadcasted_iota(jnp.int32, (1, cols), 1)
    head_of_lane = lane // bs
    blk_t = lax.broadcasted_iota(jnp.int32, (nblk, cols), 0)
    dist0 = ((n - blk_t) * bs + (lane & (bs - 1))).astype(F32)
    sub = lax.broadcasted_iota(jnp.int32, (SUBLANES, cols), 0)
    pad = jnp.zeros((MOBA_EXTRA - MOBA_SPLIT * nblk - SUBLANES, cols), F32)

    q_t = q_ref[...].T

    for kv in range(kv_heads):
        slope = jnp.zeros((1, cols), F32)
        for g in range(GQA_GROUP):
            slope = jnp.where(head_of_lane == g, slopes[kv * GQA_GROUP + g], slope)
        heads = [q_t[(kv * GQA_GROUP + g) * HEAD_DIM:(kv * GQA_GROUP + g + 1) * HEAD_DIM, :]
                 for g in range(GQA_GROUP)]
        qf_t = jnp.concatenate(heads, axis=1) * HEAD_DIM ** -0.5

        gate_t = jnp.dot(km_ref[0][:, kv * HEAD_DIM:(kv + 1) * HEAD_DIM], qf_t,
                         preferred_element_type=F32, precision=lax.Precision.HIGHEST)
        rank = jnp.zeros((nblk, cols), jnp.int32)
        for c in range(nblk):
            gc = gate_t[c:c + 1, :]
            beats = (gc > gate_t) | ((gc == gate_t) & (c < blk_t))
            rank = rank + jnp.where(beats, jnp.where(c < n, 1, 0), 0)
        chosen = ((blk_t < n) & (rank < MOBA_TOPK)) | (blk_t == n)
        bias_t = jnp.where(chosen, 0.0, NEG_INF) - slope * dist0
        slope_rows = jnp.zeros((SUBLANES, cols), F32)
        for t, term in enumerate(_split_bf16(slope)):
            slope_rows = jnp.where(sub == t, term, slope_rows)
        qa_scr[kv] = jnp.concatenate([qf_t] + _split_bf16(bias_t) + [slope_rows, pad], axis=0).astype(BF16)

    m_scr[...] = jnp.full_like(m_scr, NEG_INF)
    l_scr[...] = jnp.zeros_like(l_scr)
    acc_scr[...] = jnp.zeros_like(acc_scr)

    def attend(j0, nb, mask):
        start = pl.multiple_of(j0 * bs, bs)
        for kv in range(kv_heads):
            s = jnp.dot(ka_ref[0, kv, pl.ds(start, nb * bs), :], qa_scr[kv], preferred_element_type=F32)
            if mask is not None:
                s = jnp.where(mask, s, NEG_INF)
            m_old = m_scr[kv]
            m_new = jnp.maximum(m_old, jnp.max(s, axis=0, keepdims=True))
            alpha = jnp.exp(m_old - m_new)
            p = jnp.exp(s - m_new)
            v_t = jnp.concatenate([vt_ref[0, j0 + t, kv * HEAD_DIM:(kv + 1) * HEAD_DIM, :] for t in range(nb)],
                                  axis=1)
            acc_scr[kv] = alpha * acc_scr[kv] + jnp.dot(v_t, p.astype(BF16), preferred_element_type=F32)
            l_scr[kv] = alpha * l_scr[kv] + jnp.sum(p, axis=0, keepdims=True)
            m_scr[kv] = m_new

    tok = lax.broadcasted_iota(jnp.int32, (1, cols), 1) & (bs - 1)
    odd = (n + 1) % 2

    @pl.when(odd == 1)
    def _():
        visible = jnp.where(n > 0, bs, 0)
        key1 = lax.broadcasted_iota(jnp.int32, (bs, cols), 0)
        attend(0, 1, tok + visible >= key1)

    def body(i, carry):
        attend(odd + 2 * i, 2, None)
        return carry

    lax.fori_loop(0, (n + 1) // 2 - 1, body, 0)

    @pl.when(n >= 1)
    def _():
        key2 = lax.broadcasted_iota(jnp.int32, (2 * bs, cols), 0)
        attend(n - 1, 2, tok + bs >= key2)

    for kv in range(kv_heads):
        o_t = acc_scr[kv] / l_scr[kv]
        for g in range(GQA_GROUP):
            h = kv * GQA_GROUP + g
            ot_scr[h * HEAD_DIM:(h + 1) * HEAD_DIM, :] = o_t[:, g * bs:(g + 1) * bs]
    o_ref[...] = ot_scr[...].T


def _moba_attention(proj, batch, seq, *, width, q_col, k_col, v_col):
    n = proj.shape[0]
    kvw = width // GQA_GROUP
    kv_heads = kvw // HEAD_DIM
    bs = MOBA_BLOCK
    nblk = seq // bs
    assert MOBA_SPLIT * nblk + SUBLANES <= MOBA_EXTRA
    k_means, k_aug, v_t = _moba_prepare(proj, batch, seq, k_col=k_col, v_col=v_col, kvw=kvw)
    cols = GQA_GROUP * bs
    return pl.pallas_call(
        functools.partial(_moba_kernel, slopes=_alibi_slopes(width // HEAD_DIM)),
        grid=(batch, nblk),
        in_specs=[
            pl.BlockSpec((bs, width), lambda b, i: (b * nblk + i, q_col)),
            pl.BlockSpec((1, kv_heads, seq, HEAD_DIM + MOBA_EXTRA), lambda b, i: (b, 0, 0, 0)),
            pl.BlockSpec((1, nblk, kvw, bs), lambda b, i: (b, 0, 0, 0)),
            pl.BlockSpec((1, nblk, kvw), lambda b, i: (b, 0, 0)),
        ],
        out_specs=pl.BlockSpec((bs, width), lambda b, i: (b * nblk + i, 0)),
        out_shape=jax.ShapeDtypeStruct((n, width), F32),
        scratch_shapes=[
            pltpu.VMEM((kv_heads, HEAD_DIM + MOBA_EXTRA, cols), BF16),
            pltpu.VMEM((kv_heads, HEAD_DIM, cols), F32),
            pltpu.VMEM((kv_heads, 1, cols), F32),
            pltpu.VMEM((kv_heads, 1, cols), F32),
            pltpu.VMEM((width, bs), F32),
        ],
        compiler_params=_params("arbitrary", "arbitrary"),
        name="moba_attention",
    )(proj, k_aug, v_t, k_means)


def kernel(x, c, ada_w, ada_b, norm_mix_pre, norm_mix_post, norm_ffn_pre, norm_ffn_post, ffn_w_up, ffn_conv_w, ffn_conv_b, ffn_w_down, ev_w_in, ev_w_out, ev_sinks, ev_pool_w, ev_pool_b, ev_pool_scale, od_w_in, od_w_out, od_conv_w, od_conv_b, od_gate_b, od_mh_norm):
    batch, seq, d = x.shape
    depth = ada_w.shape[0]
    assert seq % 512 == 0
    x2 = x.reshape(batch * seq, d)

    mod = _ada_modulation(c, ada_w, ada_b)
    mod = mod.reshape(depth, batch, 6, 1, d)

    for layer in range(depth):
        sh_m, sc_m, gt_m, sh_f, sc_f, gt_f = [mod[layer, :, r] for r in range(6)]
        j = layer // 2
        if layer % 2 == 0:
            width = ev_pool_scale.shape[1]
            kvw = width // GQA_GROUP
            w_in = ev_w_in[j]
            w_in = jnp.concatenate([w_in[:, :width], w_in[:, width + 2 * kvw:],
                                    w_in[:, width:width + 2 * kvw]], axis=1).astype(BF16)
            proj = _input_projection(x2, seq, norm_mix_pre[layer], sc_m, sh_m, w_in, tm=1024, tn=1280)
            a = _sliding_window_attention(proj, ev_sinks[j], batch, seq, width=width,
                                          k_col=2 * width // kvw, v_col=2 * width // kvw + 1, tq=512)
            b = _pool_mixer(proj, ev_pool_w[j].astype(BF16), ev_pool_b[j], ev_pool_scale[j], batch, seq,
                            col=1, tt=512)
            w_out = ev_w_out[j].astype(BF16)
        else:
            width = od_mh_norm.shape[1]
            kvw = width // GQA_GROUP
            heads = MLSTM_HEADS
            w_in = od_w_in[j]
            g0 = 3 * width
            w_main = jnp.concatenate([w_in[:, :g0], w_in[:, g0 + 2 * heads:]], axis=1).astype(BF16)
            w_gates = jnp.zeros((d, LANES), BF16).at[:, :2 * heads].set(
                w_in[:, g0:g0 + 2 * heads].astype(BF16))
            proj, gates = _input_projection(x2, seq, norm_mix_pre[layer], sc_m, sh_m, w_main, w_gates,
                                            tm=1024, tn=1536)
            a = _mlstm(proj, gates, od_conv_w[j], od_conv_b[j], od_gate_b[j], od_mh_norm[j], batch, seq)
            b = _moba_attention(proj, batch, seq, width=width, q_col=3,
                                k_col=4 * width // kvw, v_col=4 * width // kvw + 1)
            w_out = od_w_out[j].astype(BF16)
        x2 = _output_projection(a, b, w_out, x2, seq, gt_m, norm_mix_post[layer], tm=512)
        x2 = _conv_ffn(x2, seq, norm_ffn_pre[layer], sc_f, sh_f, ffn_w_up[layer].astype(BF16),
                       ffn_conv_w[layer], ffn_conv_b[layer], ffn_w_down[layer].astype(BF16),
                       gt_f, norm_ffn_post[layer], tm=1024, tf=512)
    return x2.reshape(batch, seq, d)
```
